```python
import math
import jax, jax.numpy as jnp
from jax import lax
import numpy as np

D_MODEL = 1024
BATCH = 16
SEQ = 2048
DEPTH = 1

CHUNK = 64
D_MIX = D_MODEL
PLE_DIM = 256
D_FF = 2816
HG_WIDTH = D_MIX // 2
HG_HEADS = 4
HG_DK = 128
HG_DV = HG_WIDTH // HG_HEADS
HG_FDIM = HG_HEADS * HG_DK
SSD_WIDTH = D_MIX - HG_WIDTH
SSD_HEADDIM = 64
SSD_HEADS = SSD_WIDTH // SSD_HEADDIM
SSD_GROUPS = 2
SSD_HPG = SSD_HEADS // SSD_GROUPS
SSD_STATE = 128
SSD_CONV = 4
SSD_CONV_CH = SSD_WIDTH + 2 * SSD_GROUPS * SSD_STATE
MIX_COLS = (HG_FDIM, HG_FDIM, HG_WIDTH, HG_WIDTH, SSD_WIDTH, SSD_CONV_CH, SSD_HEADS)
MIX_SPLITS = tuple(int(c) for c in np.cumsum(MIX_COLS)[:-1])
MIX_TOTAL = sum(MIX_COLS)
DEEPNORM_ALPHA = (2.0 * DEPTH) ** 0.25
DEEPNORM_BETA = (8.0 * DEPTH) ** -0.25
LN_EPS = 1e-5
RMS_EPS = 1e-6

kernel_name = "hybrid_hgrn2_ssd_macaron_deepnorm_block"


def layer_norm(x, g, b):
    xf = x.astype(jnp.float32)
    mu = jnp.mean(xf, axis=-1, keepdims=True)
    var = jnp.mean(jnp.square(xf - mu), axis=-1, keepdims=True)
    return ((xf - mu) * lax.rsqrt(var + LN_EPS) * g + b).astype(x.dtype)


def rms_norm(x, g):
    xf = x.astype(jnp.float32)
    return xf * lax.rsqrt(jnp.mean(jnp.square(xf), axis=-1, keepdims=True) + RMS_EPS) * g


def swiglu(x, w_in, w_out):
    gate, up = jnp.split(x @ w_in, 2, axis=-1)
    return (jax.nn.silu(gate) * up) @ w_out


def hgrn2_mixer(q, f_raw, v, g, lb, norm_g):
    bsz, seqlen, _ = q.shape
    nc = seqlen // CHUNK
    f32 = jnp.float32
    f = lb + (1.0 - lb) * jax.nn.sigmoid(f_raw.astype(f32))
    log_f = jnp.log(f)
    k = 1.0 - f
    q = jax.nn.silu(q.astype(f32))
    v = v.astype(f32)

    def heads(t, d):
        return t.reshape(bsz, nc, CHUNK, HG_HEADS, d).transpose(1, 0, 3, 2, 4)

    qc, kc, vc = heads(q, HG_DK), heads(k, HG_DK), heads(v, HG_DV)
    bc = jnp.cumsum(heads(log_f, HG_DK), axis=3)
    causal = jnp.tril(jnp.ones((CHUNK, CHUNK), bool))

    def step(state, blk):
        qb, kb, vb, bb = blk
        diff = bb[:, :, :, None, :] - bb[:, :, None, :, :]
        decay = jnp.exp(jnp.where(causal[:, :, None], diff, -jnp.inf))
        scores = jnp.einsum('bhtk,bhtsk,bhsk->bhts', qb, decay, kb)
        o = scores @ vb + jnp.einsum('bhtk,bhkv->bhtv', qb * jnp.exp(bb), state)
        b_last = bb[:, :, -1:, :]
        state = (jnp.exp(b_last[:, :, 0, :, None]) * state
                 + jnp.einsum('bhsk,bhsv->bhkv', kb * jnp.exp(b_last - bb), vb))
        return state, o

    s0 = jnp.zeros((bsz, HG_HEADS, HG_DK, HG_DV), f32)
    _, o = lax.scan(step, s0, (qc, kc, vc, bc))
    o = o.transpose(1, 0, 3, 2, 4).reshape(bsz, seqlen, HG_HEADS, HG_DV)
    o = rms_norm(o, norm_g).reshape(bsz, seqlen, HG_WIDTH)
    return o * jax.nn.silu(g.astype(f32))


def ssd_mixer(z, xbc, dt_raw, conv_w, conv_b, dt_bias, a_log, d_skip, norm_g):
    bsz, seqlen, _ = xbc.shape
    nc = seqlen // CHUNK
    f32 = jnp.float32
    xpad = jnp.pad(xbc.astype(f32), ((0, 0), (SSD_CONV - 1, 0), (0, 0)))
    conv = conv_b + sum(conv_w[j] * xpad[:, j:j + seqlen] for j in range(SSD_CONV))
    xbc = jax.nn.silu(conv)
    xs, bm, cm = jnp.split(xbc, (SSD_WIDTH, SSD_WIDTH + SSD_GROUPS * SSD_STATE), axis=-1)
    xs = xs.reshape(bsz, nc, CHUNK, SSD_GROUPS, SSD_HPG, SSD_HEADDIM)
    bm = bm.reshape(bsz, nc, CHUNK, SSD_GROUPS, SSD_STATE)
    cm = cm.reshape(bsz, nc, CHUNK, SSD_GROUPS, SSD_STATE)
    dt = jax.nn.softplus(dt_raw.astype(f32) + dt_bias).reshape(bsz, nc, CHUNK, SSD_GROUPS, SSD_HPG)
    a = dt * (-jnp.exp(a_log.astype(f32))).reshape(SSD_GROUPS, SSD_HPG)
    a_cum = jnp.cumsum(a.transpose(0, 3, 4, 1, 2), axis=-1)
    causal = jnp.tril(jnp.ones((CHUNK, CHUNK), bool))
    seg = a_cum[..., :, None] - a_cum[..., None, :]
    lmat = jnp.exp(jnp.where(causal, seg, -jnp.inf))
    dtx = xs * dt[..., None]
    cb = jnp.einsum('bctgn,bcsgn->bgcts', cm, bm)
    y_diag = jnp.einsum('bgcts,bgrcts,bcsgrp->bctgrp', cb, lmat, dtx)
    decay_end = jnp.exp(a_cum[..., -1:] - a_cum)
    states = jnp.einsum('bcsgn,bgrcs,bcsgrp->cbgrpn', bm, decay_end, dtx)
    chunk_decay = jnp.moveaxis(jnp.exp(a_cum[..., -1]), -1, 0)

    def pass_state(h, inp):
        dec, st = inp
        return dec[..., None, None] * h + st, h

    h0 = jnp.zeros((bsz, SSD_GROUPS, SSD_HPG, SSD_HEADDIM, SSD_STATE), f32)
    _, h_prev = lax.scan(pass_state, h0, (chunk_decay, states))
    y_off = jnp.einsum('bctgn,bgrct,cbgrpn->bctgrp', cm, jnp.exp(a_cum), h_prev)
    y = y_diag + y_off + d_skip.astype(f32).reshape(SSD_GROUPS, SSD_HPG, 1) * xs
    y = y.reshape(bsz, seqlen, SSD_WIDTH)
    return rms_norm(y * jax.nn.silu(z.astype(f32)), norm_g)


def setup_inputs(seed: int = 0) -> dict:
    key = jax.random.key(seed)
    ks = jax.random.split(key, 32)
    f32 = jnp.float32

    def nrm(k, shape, scale):
        return scale * jax.random.normal(k, shape, f32)

    def gain(k, shape):
        return 1.0 + nrm(k, shape, 0.02)

    dt = jnp.exp(jax.random.uniform(ks[13], (DEPTH, SSD_HEADS), f32, math.log(1e-3), math.log(1e-1)))
    return {
        "x": nrm(ks[0], (BATCH, SEQ, D_MODEL), 1.0),
        "p": nrm(ks[1], (DEPTH, BATCH, SEQ, PLE_DIM), 1.0),
        "ffn1_w_in": nrm(ks[2], (DEPTH, D_MODEL, 2 * D_FF), D_MODEL ** -0.5),
        "ffn1_w_out": nrm(ks[3], (DEPTH, D_FF, D_MODEL), DEEPNORM_BETA * D_FF ** -0.5),
        "ln1_g": gain(ks[4], (DEPTH, D_MODEL)),
        "ln1_b": nrm(ks[5], (DEPTH, D_MODEL), 0.02),
        "w_in_mix": nrm(ks[6], (DEPTH, D_MODEL, MIX_TOTAL), D_MODEL ** -0.5),
        "hgrn_lb_logits": nrm(ks[7], (DEPTH + 1, HG_FDIM), 0.1),
        "hgrn_norm_g": gain(ks[8], (DEPTH, HG_DV)),
        "ssd_conv_w": nrm(ks[9], (DEPTH, SSD_CONV, SSD_CONV_CH), SSD_CONV ** -0.5),
        "ssd_conv_b": nrm(ks[10], (DEPTH, SSD_CONV_CH), 0.02),
        "ssd_dt_bias": dt + jnp.log(-jnp.expm1(-dt)),
        "ssd_a_log": jnp.log(jax.random.uniform(ks[11], (DEPTH, SSD_HEADS), f32, 1.0, 16.0)),
        "ssd_d": gain(ks[12], (DEPTH, SSD_HEADS)),
        "ssd_norm_g": gain(ks[14], (DEPTH, SSD_WIDTH)),
        "w_out_mix": nrm(ks[15], (DEPTH, D_MIX, D_MODEL), DEEPNORM_BETA * D_MIX ** -0.5),
        "ln2_g": gain(ks[16], (DEPTH, D_MODEL)),
        "ln2_b": nrm(ks[17], (DEPTH, D_MODEL), 0.02),
        "ffn2_w_in": nrm(ks[18], (DEPTH, D_MODEL, 2 * D_FF), D_MODEL ** -0.5),
        "ffn2_w_out": nrm(ks[19], (DEPTH, D_FF, D_MODEL), DEEPNORM_BETA * D_FF ** -0.5),
        "ln3_g": gain(ks[20], (DEPTH, D_MODEL)),
        "ln3_b": nrm(ks[21], (DEPTH, D_MODEL), 0.02),
        "ple_w_proj": nrm(ks[22], (DEPTH, PLE_DIM, D_MODEL), DEEPNORM_BETA * PLE_DIM ** -0.5),
        "ple_w_gate": nrm(ks[23], (DEPTH, D_MODEL, D_MODEL), D_MODEL ** -0.5),
        "ln4_g": gain(ks[24], (DEPTH, D_MODEL)),
        "ln4_b": nrm(ks[25], (DEPTH, D_MODEL), 0.02),
    }


def reference(x, p, ffn1_w_in, ffn1_w_out, ln1_g, ln1_b, w_in_mix, hgrn_lb_logits, hgrn_norm_g,
              ssd_conv_w, ssd_conv_b, ssd_dt_bias, ssd_a_log, ssd_d, ssd_norm_g, w_out_mix,
              ln2_g, ln2_b, ffn2_w_in, ffn2_w_out, ln3_g, ln3_b, ple_w_proj, ple_w_gate,
              ln4_g, ln4_b):
    lb_all = jnp.cumsum(jax.nn.softmax(hgrn_lb_logits.astype(jnp.float32), axis=0), axis=0)
    for i in range(DEPTH):
        x = layer_norm(DEEPNORM_ALPHA * x + 0.5 * swiglu(x, ffn1_w_in[i], ffn1_w_out[i]), ln1_g[i], ln1_b[i])
        hq, hf, hi, hg, sz, sxbc, sdt = jnp.split(x @ w_in_mix[i], MIX_SPLITS, axis=-1)
        o_hgrn = hgrn2_mixer(hq, hf, hi, hg, lb_all[i], hgrn_norm_g[i])
        o_ssd = ssd_mixer(sz, sxbc, sdt, ssd_conv_w[i], ssd_conv_b[i], ssd_dt_bias[i],
                          ssd_a_log[i], ssd_d[i], ssd_norm_g[i])
        mix = jnp.concatenate([o_hgrn, o_ssd], axis=-1) @ w_out_mix[i]
        x = layer_norm(DEEPNORM_ALPHA * x + mix, ln2_g[i], ln2_b[i])
        x = layer_norm(DEEPNORM_ALPHA * x + 0.5 * swiglu(x, ffn2_w_in[i], ffn2_w_out[i]), ln3_g[i], ln3_b[i])
        ple = jax.nn.sigmoid(x @ ple_w_gate[i]) * (p[i] @ ple_w_proj[i])
        x = layer_norm(DEEPNORM_ALPHA * x + ple, ln4_g[i], ln4_b[i])
    return x
```

```python
import functools

import jax
import jax.numpy as jnp
from jax import lax
from jax.experimental import pallas as pl
from jax.experimental.pallas import tpu as pltpu

F32 = jnp.float32
BF16 = jnp.bfloat16

DEPTH = 1
D_MODEL = 1024
D_FF = 2816
CHUNK = 64
PLE_DIM = 256
HG_HEADS = 4
HG_DK = 128
HG_DV = 128
HG_WIDTH = HG_HEADS * HG_DV
HG_FDIM = HG_HEADS * HG_DK
SSD_WIDTH = 512
SSD_HEADDIM = 64
SSD_HEADS = SSD_WIDTH // SSD_HEADDIM
SSD_GROUPS = 2
SSD_HPG = SSD_HEADS // SSD_GROUPS
SSD_STATE = 128
SSD_CONV = 4
SSD_CONV_CH = SSD_WIDTH + 2 * SSD_GROUPS * SSD_STATE
SSD_GROUP_W = SSD_HPG * SSD_HEADDIM
DEEPNORM_ALPHA = (2.0 * DEPTH) ** 0.25
LN_EPS = 1e-5
RMS_EPS = 1e-6

LANES = 128
SUBLANES = 8
V7X_VMEM_LIMIT_BYTES = 56 * 1024 * 1024

DT_PAD = LANES
MIX_COLS_PADDED = 2 * HG_FDIM + 2 * HG_WIDTH + SSD_WIDTH + SSD_CONV_CH + DT_PAD
TOKEN_TILE = 256
FF_CHUNK = D_FF // 2
SEQ_TILE = 256
SUB = 16
NSUB = CHUNK // SUB
NEG_BIG = -1e30


def _sigmoid(x):
    return 1.0 / (1.0 + jnp.exp(-x))


def _silu(x):
    return x * _sigmoid(x)


def _softplus(x):
    return jnp.maximum(x, 0.0) + jnp.log1p(jnp.exp(-jnp.abs(x)))


def _dot(a, b):
    return jnp.dot(a, b, preferred_element_type=F32)


def _dot_nt(a, b):
    return lax.dot_general(a, b, (((1,), (1,)), ((), ())), preferred_element_type=F32)


def _dot_tn(a, b):
    return lax.dot_general(a, b, (((0,), (0,)), ((), ())), preferred_element_type=F32)


def _split3(x):
    hi = x.astype(BF16)
    r1 = x - hi.astype(F32)
    mid = r1.astype(BF16)
    lo = (r1 - mid.astype(F32)).astype(BF16)
    return hi, mid, lo


def _dot_sel(sel, x):
    hi, mid, lo = _split3(x)
    return _dot(sel, hi) + _dot(sel, mid) + _dot(sel, lo)


def _dot_x_sel(x, sel):
    hi, mid, lo = _split3(x)
    return _dot(hi, sel) + _dot(mid, sel) + _dot(lo, sel)


def _dot_tn_sel(x, sel):
    hi, mid, lo = _split3(x)
    return _dot_tn(hi, sel) + _dot_tn(mid, sel) + _dot_tn(lo, sel)


def _layer_norm(y, g, b):
    mu = jnp.mean(y, axis=-1, keepdims=True)
    d = y - mu
    var = jnp.mean(d * d, axis=-1, keepdims=True)
    return d * lax.rsqrt(var + LN_EPS) * g + b


def _ffn_ln(x, w_in_ref, w_out_ref, g, b):
    xb = x.astype(BF16)
    acc = None
    for c0 in range(0, D_FF, FF_CHUNK):
        gate = _dot(xb, w_in_ref[:, c0:c0 + FF_CHUNK])
        up = _dot(xb, w_in_ref[:, D_FF + c0:D_FF + c0 + FF_CHUNK])
        act = (_silu(gate) * up).astype(BF16)
        part = _dot(act, w_out_ref[c0:c0 + FF_CHUNK, :])
        acc = part if acc is None else acc + part
    return _layer_norm(DEEPNORM_ALPHA * x + 0.5 * acc, g, b)


def _pre_kernel(x_ref, w1i_ref, w1o_ref, g1_ref, b1_ref, wmix_ref,
                x1_ref, hq_ref, hf_ref, hi_ref, hg_ref, sz_ref, sxbc_ref, sdt_ref):
    x1 = _ffn_ln(x_ref[...], w1i_ref, w1o_ref, g1_ref[...], b1_ref[...])
    x1_ref[...] = x1
    u = _dot(x1.astype(BF16), wmix_ref[...])
    c = 0
    for ref in (hq_ref, hf_ref, hi_ref, hg_ref, sz_ref, sxbc_ref, sdt_ref):
        w = ref.shape[-1]
        ref[...] = u[:, c:c + w]
        c += w


def _resident(shape):
    return pl.BlockSpec(shape, lambda *_: (0,) * len(shape), pipeline_mode=pl.Buffered(1))


def _rows(width):
    return pl.BlockSpec((TOKEN_TILE, width), lambda i: (i, 0))


def _pre_call(x2d, w1i, w1o, g1, b1, wmix):
    n = x2d.shape[0]
    widths = (HG_FDIM, HG_FDIM, HG_WIDTH, HG_WIDTH, SSD_WIDTH, SSD_CONV_CH, DT_PAD)
    out_shape = [jax.ShapeDtypeStruct((n, D_MODEL), F32)] + [
        jax.ShapeDtypeStruct((n, w), F32) for w in widths]
    return pl.pallas_call(
        _pre_kernel,
        grid=(n // TOKEN_TILE,),
        in_specs=[_rows(D_MODEL), _resident(w1i.shape), _resident(w1o.shape),
                  _resident(g1.shape), _resident(b1.shape), _resident(wmix.shape)],
        out_specs=[_rows(D_MODEL)] + [_rows(w) for w in widths],
        out_shape=out_shape,
        compiler_params=pltpu.CompilerParams(
            dimension_semantics=("arbitrary",), vmem_limit_bytes=V7X_VMEM_LIMIT_BYTES),
        name="pre_ffn_inproj",
    )(x2d, w1i, w1o, g1, b1, wmix)


def _hgrn_kernel(lbl_ref, ng_ref, q_ref, f_ref, v_ref, g_ref, o_ref,
                 st_ref, b_sc, kk_sc, p_sc):
    @pl.when(pl.program_id(1) == 0)
    def _():
        st_ref[...] = jnp.zeros_like(st_ref)

    logits = lbl_ref[...]
    ex = jnp.exp(logits - jnp.max(logits, axis=0, keepdims=True))
    lb = ex[0:1, :] / jnp.sum(ex, axis=0, keepdims=True)
    ng = ng_ref[...]

    row = lax.broadcasted_iota(jnp.int32, (CHUNK, CHUNK), 0)
    col = lax.broadcasted_iota(jnp.int32, (CHUNK, CHUNK), 1)
    tri = (col <= row).astype(BF16)
    diag_mask = (col // SUB == row // SUB) & (col <= row)
    srow = lax.broadcasted_iota(jnp.int32, (SUB * HG_DK, CHUNK), 0)
    scol = lax.broadcasted_iota(jnp.int32, (SUB * HG_DK, CHUNK), 1)
    sumsel = (srow // HG_DK == scol % SUB).astype(BF16)
    rowk = lax.broadcasted_iota(jnp.int32, (CHUNK, HG_DK), 0)

    def chunk_body(c, carry):
        r0 = pl.multiple_of(c * CHUNK, CHUNK)
        rows = pl.ds(r0, CHUNK)
        f = lb + (1.0 - lb) * _sigmoid(f_ref[0, rows, :])
        b = _dot_sel(tri, jnp.log(f))
        kk = 1.0 - f
        qq = _silu(q_ref[0, rows, :])
        b_sc[...] = b
        kk_sc[...] = kk

        for h in range(HG_HEADS):
            hs = slice(h * HG_DK, (h + 1) * HG_DK)
            for i in range(NSUB):
                bt = b[i * SUB:(i + 1) * SUB, hs]
                qt = qq[i * SUB:(i + 1) * SUB, hs]
                for s in range(SUB):
                    r = i * SUB + s
                    bs = b_sc[r:r + 1, hs]
                    ks = kk_sc[r:r + 1, hs]
                    piece = qt * jnp.exp(jnp.minimum(bt - bs, 0.0)) * ks
                    p_sc[h * CHUNK + i * SUB:h * CHUNK + (i + 1) * SUB,
                         s * HG_DK:(s + 1) * HG_DK] = piece.astype(BF16)
        a_diag = _dot(p_sc[...], sumsel)

        for h in range(HG_HEADS):
            hs = slice(h * HG_DK, (h + 1) * HG_DK)
            bh, qh, kh = b[:, hs], qq[:, hs], kk[:, hs]
            vh = v_ref[0, rows, hs].astype(BF16)
            lhs, rhs = [], []
            for i in range(1, NSUB):
                bref = bh[i * SUB - 1:i * SUB, :]
                in_blk = (rowk >= i * SUB) & (rowk < (i + 1) * SUB)
                lhs.append(qh * jnp.exp(jnp.where(in_blk, bh - bref, NEG_BIG)))
                rhs.append(kh * jnp.exp(jnp.where(rowk < i * SUB, bref - bh, NEG_BIG)))
            a_inter = _dot_nt(jnp.concatenate(lhs, axis=1).astype(BF16),
                              jnp.concatenate(rhs, axis=1).astype(BF16))
            a = a_inter + jnp.where(diag_mask, a_diag[h * CHUNK:(h + 1) * CHUNK, :], 0.0)
            st = st_ref[h]
            o = _dot(a.astype(BF16), vh) + _dot_nt((qh * jnp.exp(bh)).astype(BF16),
                                                    st.astype(BF16))
            b_last = bh[CHUNK - 1:CHUNK, :]
            kd = (kh * jnp.exp(b_last - bh)).astype(BF16)
            st_ref[h] = st * jnp.exp(b_last) + _dot_tn(vh, kd)
            o = o * lax.rsqrt(jnp.mean(o * o, axis=-1, keepdims=True) + RMS_EPS) * ng
            o = o * _silu(g_ref[0, rows, hs])
            o_ref[0, rows, hs] = o.astype(o_ref.dtype)
        return carry

    lax.fori_loop(0, SEQ_TILE // CHUNK, chunk_body, 0)


def _seq_spec(width):
    return pl.BlockSpec((1, SEQ_TILE, width), lambda b, t: (b, t, 0))


def _hgrn_call(lb_logits, norm_g, hq, hf, hi, hg):
    bsz, seqlen, _ = hq.shape
    return pl.pallas_call(
        _hgrn_kernel,
        grid=(bsz, seqlen // SEQ_TILE),
        in_specs=[_resident(lb_logits.shape), _resident(norm_g.shape),
                  _seq_spec(HG_FDIM), _seq_spec(HG_FDIM), _seq_spec(HG_WIDTH), _seq_spec(HG_WIDTH)],
        out_specs=_seq_spec(HG_WIDTH),
        out_shape=jax.ShapeDtypeStruct((bsz, seqlen, HG_WIDTH), BF16),
        scratch_shapes=[
            pltpu.VMEM((HG_HEADS, HG_DV, HG_DK), F32),
            pltpu.VMEM((CHUNK, HG_FDIM), F32),
            pltpu.VMEM((CHUNK, HG_FDIM), F32),
            pltpu.VMEM((HG_HEADS * CHUNK, SUB * HG_DK), BF16),
        ],
        compiler_params=pltpu.CompilerParams(
            dimension_semantics=("arbitrary", "arbitrary"), vmem_limit_bytes=V7X_VMEM_LIMIT_BYTES),
        name="hgrn2_mixer",
    )(lb_logits, norm_g, hq, hf, hi, hg)


def _ssd_kernel(cw_ref, cb_ref, dtb_ref, alog_ref, dexp_ref, ng_ref,
                z_ref, xbc_ref, dt_ref, o_ref,
                ht_ref, xbuf, xc_sc, dtx_sc, aexp_sc, a_sc):
    @pl.when(pl.program_id(1) == 0)
    def _():
        ht_ref[...] = jnp.zeros_like(ht_ref)
        xbuf[0:SUBLANES, :] = jnp.zeros((SUBLANES, SSD_CONV_CH), F32)

    xbuf[SUBLANES:SUBLANES + SEQ_TILE, :] = xbc_ref[0]
    conv = cb_ref[...]
    for j in range(SSD_CONV):
        off = SUBLANES - (SSD_CONV - 1) + j
        conv = conv + cw_ref[j:j + 1, :] * xbuf[off:off + SEQ_TILE, :]
    xc_sc[...] = _silu(conv)
    xbuf[0:SUBLANES, :] = xbuf[SEQ_TILE:SEQ_TILE + SUBLANES, :]

    hrow = lax.broadcasted_iota(jnp.int32, (DT_PAD, SSD_WIDTH), 0)
    hcol = lax.broadcasted_iota(jnp.int32, (DT_PAD, SSD_WIDTH), 1)
    expand = (hrow == hcol // SSD_HEADDIM).astype(BF16)
    dt = _softplus(dt_ref[0] + dtb_ref[...])
    a = dt * (-jnp.exp(alog_ref[...]))
    a_sc[...] = a
    dtx_sc[...] = xc_sc[:, 0:SSD_WIDTH] * _dot_x_sel(dt, expand)
    aexp_sc[...] = _dot_x_sel(a, expand)

    row = lax.broadcasted_iota(jnp.int32, (CHUNK, CHUNK), 0)
    col = lax.broadcasted_iota(jnp.int32, (CHUNK, CHUNK), 1)
    causal = col <= row
    tri = causal.astype(BF16)
    upper = (row <= col).astype(BF16)
    dexp = dexp_ref[...]
    ng = ng_ref[...]

    def chunk_body(c, carry):
        r0 = pl.multiple_of(c * CHUNK, CHUNK)
        rows = pl.ds(r0, CHUNK)
        xs = xc_sc[rows, 0:SSD_WIDTH]
        dtx = dtx_sc[rows, :]
        a_exp = aexp_sc[rows, :]
        acum = _dot_sel(tri, a_exp)
        acum_t = _dot_tn_sel(a_sc[rows, :], upper)
        a_last = acum[CHUNK - 1:CHUNK, :]
        dtx_end = (dtx * jnp.exp(a_last - acum)).astype(BF16)
        dtxb = dtx.astype(BF16)
        e_cum = jnp.exp(acum)
        e_last = jnp.exp(a_last)

        ys = []
        for g in range(SSD_GROUPS):
            gs = slice(g * SSD_GROUP_W, (g + 1) * SSD_GROUP_W)
            b0 = SSD_WIDTH + g * SSD_STATE
            c0 = SSD_WIDTH + SSD_GROUPS * SSD_STATE + g * SSD_STATE
            bm = xc_sc[rows, b0:b0 + SSD_STATE].astype(BF16)
            cm = xc_sc[rows, c0:c0 + SSD_STATE].astype(BF16)
            cbm = _dot_nt(cm, bm)
            ht = ht_ref[g]
            y_off = _dot(cm, ht.astype(BF16)) * e_cum[:, gs]
            y_diag = []
            for r in range(SSD_HPG):
                h = g * SSD_HPG + r
                hsl = slice(h * SSD_HEADDIM, (h + 1) * SSD_HEADDIM)
                seg = acum[:, hsl] - acum_t[h:h + 1, :]
                lmat = jnp.exp(jnp.where(causal, seg, NEG_BIG))
                y_diag.append(_dot((cbm * lmat).astype(BF16), dtxb[:, hsl]))
            ys.append(jnp.concatenate(y_diag, axis=1) + y_off)
            ht_ref[g] = ht * e_last[:, gs] + _dot_tn(bm, dtx_end[:, gs])
        y = jnp.concatenate(ys, axis=1) + dexp * xs
        y = y * _silu(z_ref[0, rows, :])
        y = y * lax.rsqrt(jnp.mean(y * y, axis=-1, keepdims=True) + RMS_EPS) * ng
        o_ref[0, rows, :] = y.astype(o_ref.dtype)
        return carry

    lax.fori_loop(0, SEQ_TILE // CHUNK, chunk_body, 0)


def _ssd_call(conv_w, conv_b, dt_bias, a_log, d_exp, norm_g, sz, sxbc, sdt):
    bsz, seqlen, _ = sz.shape
    params = (conv_w, conv_b, dt_bias, a_log, d_exp, norm_g)
    return pl.pallas_call(
        _ssd_kernel,
        grid=(bsz, seqlen // SEQ_TILE),
        in_specs=[_resident(p.shape) for p in params]
        + [_seq_spec(SSD_WIDTH), _seq_spec(SSD_CONV_CH), _seq_spec(DT_PAD)],
        out_specs=_seq_spec(SSD_WIDTH),
        out_shape=jax.ShapeDtypeStruct((bsz, seqlen, SSD_WIDTH), BF16),
        scratch_shapes=[
            pltpu.VMEM((SSD_GROUPS, SSD_STATE, SSD_GROUP_W), F32),
            pltpu.VMEM((SEQ_TILE + SUBLANES, SSD_CONV_CH), F32),
            pltpu.VMEM((SEQ_TILE, SSD_CONV_CH), F32),
            pltpu.VMEM((SEQ_TILE, SSD_WIDTH), F32),
            pltpu.VMEM((SEQ_TILE, SSD_WIDTH), F32),
            pltpu.VMEM((SEQ_TILE, DT_PAD), F32),
        ],
        compiler_params=pltpu.CompilerParams(
            dimension_semantics=("arbitrary", "arbitrary"), vmem_limit_bytes=V7X_VMEM_LIMIT_BYTES),
        name="ssd_mixer",
    )(*params, sz, sxbc, sdt)


def _post_kernel(oh_ref, os_ref, x1_ref, p_ref, wo_ref, g2_ref, b2_ref,
                 w2i_ref, w2o_ref, g3_ref, b3_ref, wg_ref, wp_ref, g4_ref, b4_ref, out_ref):
    mix = _dot(oh_ref[...], wo_ref[0:HG_WIDTH, :]) + _dot(os_ref[...], wo_ref[HG_WIDTH:, :])
    x2 = _layer_norm(DEEPNORM_ALPHA * x1_ref[...] + mix, g2_ref[...], b2_ref[...])
    x3 = _ffn_ln(x2, w2i_ref, w2o_ref, g3_ref[...], b3_ref[...])
    gate = _sigmoid(_dot(x3.astype(BF16), wg_ref[...]))
    ple = gate * _dot(p_ref[...].astype(BF16), wp_ref[...])
    out_ref[...] = _layer_norm(DEEPNORM_ALPHA * x3 + ple, g4_ref[...], b4_ref[...])


def _post_call(oh, osd, x1, p2d, wo, g2, b2, w2i, w2o, g3, b3, wg, wp, g4, b4):
    n = x1.shape[0]
    weights = (wo, g2, b2, w2i, w2o, g3, b3, wg, wp, g4, b4)
    return pl.pallas_call(
        _post_kernel,
        grid=(n // TOKEN_TILE,),
        in_specs=[_rows(HG_WIDTH), _rows(SSD_WIDTH), _rows(D_MODEL), _rows(PLE_DIM)]
        + [_resident(w.shape) for w in weights],
        out_specs=_rows(D_MODEL),
        out_shape=jax.ShapeDtypeStruct((n, D_MODEL), F32),
        compiler_params=pltpu.CompilerParams(
            dimension_semantics=("arbitrary",), vmem_limit_bytes=V7X_VMEM_LIMIT_BYTES),
        name="post_outproj_ffn_ple",
    )(oh, osd, x1, p2d, *weights)


def _row(v):
    return v.reshape(1, -1).astype(F32)


def kernel(x, p, ffn1_w_in, ffn1_w_out, ln1_g, ln1_b, w_in_mix, hgrn_lb_logits, hgrn_norm_g,
           ssd_conv_w, ssd_conv_b, ssd_dt_bias, ssd_a_log, ssd_d, ssd_norm_g, w_out_mix,
           ln2_g, ln2_b, ffn2_w_in, ffn2_w_out, ln3_g, ln3_b, ple_w_proj, ple_w_gate,
           ln4_g, ln4_b):
    bsz, seqlen, _ = x.shape
    n = bsz * seqlen
    assert DEPTH == 1 and n % TOKEN_TILE == 0 and seqlen % SEQ_TILE == 0
    h = x.reshape(n, D_MODEL)
    for i in range(DEPTH):
        pad = MIX_COLS_PADDED - w_in_mix.shape[-1]
        wmix = jnp.pad(w_in_mix[i], ((0, 0), (0, pad))).astype(BF16)
        x1, hq, hf, hi, hg, sz, sxbc, sdt = _pre_call(
            h, ffn1_w_in[i].astype(BF16), ffn1_w_out[i].astype(BF16),
            _row(ln1_g[i]), _row(ln1_b[i]), wmix)

        def seq(t):
            return t.reshape(bsz, seqlen, t.shape[-1])

        o_h = _hgrn_call(hgrn_lb_logits.astype(F32), _row(hgrn_norm_g[i]),
                         seq(hq), seq(hf), seq(hi), seq(hg))
        head_pad = DT_PAD - SSD_HEADS
        o_s = _ssd_call(
            ssd_conv_w[i].astype(F32), _row(ssd_conv_b[i]),
            jnp.pad(_row(ssd_dt_bias[i]), ((0, 0), (0, head_pad))),
            jnp.pad(_row(ssd_a_log[i]), ((0, 0), (0, head_pad))),
            _row(jnp.repeat(ssd_d[i], SSD_HEADDIM)), _row(ssd_norm_g[i]),
            seq(sz), seq(sxbc), seq(sdt))
        h = _post_call(
            o_h.reshape(n, HG_WIDTH), o_s.reshape(n, SSD_WIDTH), x1, p[i].reshape(n, PLE_DIM),
            w_out_mix[i].astype(BF16), _row(ln2_g[i]), _row(ln2_b[i]),
            ffn2_w_in[i].astype(BF16), ffn2_w_out[i].astype(BF16), _row(ln3_g[i]), _row(ln3_b[i]),
            ple_w_gate[i].astype(BF16), ple_w_proj[i].astype(BF16), _row(ln4_g[i]), _row(ln4_b[i]))
    return h.reshape(bsz, seqlen, D_MODEL)
```

```python
import jax
import jax.numpy as jnp
from jax import lax
from jax.experimental import pallas as pl
from jax.experimental.pallas import tpu as pltpu

F32 = jnp.float32
BF16 = jnp.bfloat16

DEPTH = 1
D_MODEL = 1024
D_FF = 2816
CHUNK = 64
PLE_DIM = 256
HG_HEADS = 4
HG_DK = 128
HG_DV = 128
HG_WIDTH = HG_HEADS * HG_DV
HG_FDIM = HG_HEADS * HG_DK
SSD_WIDTH = 512
SSD_HEADDIM = 64
SSD_HEADS = SSD_WIDTH // SSD_HEADDIM
SSD_GROUPS = 2
SSD_HPG = SSD_HEADS // SSD_GROUPS
SSD_STATE = 128
SSD_CONV = 4
SSD_CONV_CH = SSD_WIDTH + 2 * SSD_GROUPS * SSD_STATE
SSD_GROUP_W = SSD_HPG * SSD_HEADDIM
DEEPNORM_ALPHA = (2.0 * DEPTH) ** 0.25
LN_EPS = 1e-5
RMS_EPS = 1e-6
LOG2_E = 1.4426950408889634

LANES = 128
SUBLANES = 8
V7X_VMEM_LIMIT_BYTES = 56 * 1024 * 1024

DT_PAD = LANES
MIX_COLS_PADDED = 2 * HG_FDIM + 2 * HG_WIDTH + SSD_WIDTH + SSD_CONV_CH + DT_PAD
TOKEN_TILE = 256
FF_CHUNK = D_FF // 2
SEQ_TILE = 256
SUB = 16
NSUB = CHUNK // SUB
HALF = SUB // 2
NEG_BIG = -1e30
HGRN_GROUP = 1
SSD_GROUP = 1
SSD_Q = 128


def _sigmoid(x):
    return 1.0 / (1.0 + jnp.exp(-x))


def _silu(x):
    return x * _sigmoid(x)


def _softplus(x):
    return jnp.maximum(x, 0.0) + jnp.log1p(jnp.exp(-jnp.abs(x)))


def _dot(a, b):
    return jnp.dot(a, b, preferred_element_type=F32)


def _dot_nt(a, b):
    return lax.dot_general(a, b, (((1,), (1,)), ((), ())), preferred_element_type=F32)


def _dot_tn(a, b):
    return lax.dot_general(a, b, (((0,), (0,)), ((), ())), preferred_element_type=F32)


def _split3(x):
    hi = x.astype(BF16)
    r1 = x - hi.astype(F32)
    mid = r1.astype(BF16)
    lo = (r1 - mid.astype(F32)).astype(BF16)
    return hi, mid, lo


def _dot_sel(sel, x):
    hi, mid, lo = _split3(x)
    return _dot(sel, hi) + _dot(sel, mid) + _dot(sel, lo)


def _dot_x_sel(x, sel):
    hi, mid, lo = _split3(x)
    return _dot(hi, sel) + _dot(mid, sel) + _dot(lo, sel)


def _layer_norm(y, g, b):
    mu = jnp.mean(y, axis=-1, keepdims=True)
    d = y - mu
    var = jnp.mean(d * d, axis=-1, keepdims=True)
    return d * lax.rsqrt(var + LN_EPS) * g + b


def _ffn_ln(x, w_in_ref, w_out_ref, g, b):
    xb = x.astype(BF16)
    acc = None
    for c0 in range(0, D_FF, FF_CHUNK):
        gate = _dot(xb, w_in_ref[:, c0:c0 + FF_CHUNK])
        up = _dot(xb, w_in_ref[:, D_FF + c0:D_FF + c0 + FF_CHUNK])
        act = (_silu(gate) * up).astype(BF16)
        part = _dot(act, w_out_ref[c0:c0 + FF_CHUNK, :])
        acc = part if acc is None else acc + part
    return _layer_norm(DEEPNORM_ALPHA * x + 0.5 * acc, g, b)


def _pre_kernel(x_ref, w1i_ref, w1o_ref, g1_ref, b1_ref, wmix_ref,
                x1_ref, hq_ref, hf_ref, hi_ref, hg_ref, sz_ref, sxbc_ref, sdt_ref):
    x1 = _ffn_ln(x_ref[...], w1i_ref, w1o_ref, g1_ref[...], b1_ref[...])
    x1_ref[...] = x1
    u = _dot(x1.astype(BF16), wmix_ref[...])
    c = 0
    for ref in (hq_ref, hf_ref, hi_ref, hg_ref, sz_ref, sxbc_ref, sdt_ref):
        w = ref.shape[-1]
        ref[...] = u[:, c:c + w]
        c += w


def _resident(shape):
    return pl.BlockSpec(shape, lambda *_: (0,) * len(shape), pipeline_mode=pl.Buffered(1))


def _rows(width):
    return pl.BlockSpec((TOKEN_TILE, width), lambda i: (i, 0))


def _pre_call(x2d, w1i, w1o, g1, b1, wmix):
    n = x2d.shape[0]
    widths = (HG_FDIM, HG_FDIM, HG_WIDTH, HG_WIDTH, SSD_WIDTH, SSD_CONV_CH, DT_PAD)
    out_shape = [jax.ShapeDtypeStruct((n, D_MODEL), F32)] + [
        jax.ShapeDtypeStruct((n, w), F32) for w in widths]
    return pl.pallas_call(
        _pre_kernel,
        grid=(n // TOKEN_TILE,),
        in_specs=[_rows(D_MODEL), _resident(w1i.shape), _resident(w1o.shape),
                  _resident(g1.shape), _resident(b1.shape), _resident(wmix.shape)],
        out_specs=[_rows(D_MODEL)] + [_rows(w) for w in widths],
        out_shape=out_shape,
        compiler_params=pltpu.CompilerParams(
            dimension_semantics=("arbitrary",), vmem_limit_bytes=V7X_VMEM_LIMIT_BYTES),
        name="pre_ffn_inproj",
    )(x2d, w1i, w1o, g1, b1, wmix)


def _hgrn_kernel(lbl_ref, ng_ref, q_ref, f_ref, v_ref, g_ref, o_ref,
                 st_ref, f_sc, kk_sc, p_sc):
    @pl.when(pl.program_id(1) == 0)
    def _():
        st_ref[...] = jnp.zeros_like(st_ref)

    logits = lbl_ref[...]
    ex = jnp.exp(logits - jnp.max(logits, axis=0, keepdims=True))
    lb = ex[0:1, :] / jnp.sum(ex, axis=0, keepdims=True)
    ng = ng_ref[...]

    row = lax.broadcasted_iota(jnp.int32, (CHUNK, CHUNK), 0)
    col = lax.broadcasted_iota(jnp.int32, (CHUNK, CHUNK), 1)
    same_blk = col // SUB == row // SUB
    suffix = (same_blk & (col > row)).astype(BF16)
    diag_mask = same_blk & (col <= row)
    srow = lax.broadcasted_iota(jnp.int32, (SUB * HG_DK, CHUNK), 0)
    scol = lax.broadcasted_iota(jnp.int32, (SUB * HG_DK, CHUNK), 1)
    sumsel = (srow // HG_DK == scol % SUB).astype(BF16)
    tloc_lo = lax.broadcasted_iota(jnp.int32, (HALF, HG_DK), 0)
    tloc_hi = tloc_lo + HALF
    zero_half = jnp.zeros((HALF, HG_DK), F32)
    zero_blk = jnp.zeros((SUB, HG_DK), BF16)

    def blk(x, i):
        return x[i * SUB:(i + 1) * SUB, :]

    def chunk_body(c, carry):
        r0 = pl.multiple_of(c * CHUNK, CHUNK)
        rows = pl.ds(r0, CHUNK)
        for gi in range(HGRN_GROUP):
            f = lb + (1.0 - lb) * _sigmoid(f_ref[gi, rows, :])
            kk = 1.0 - f
            qq = _silu(q_ref[gi, rows, :])
            f_sc[gi] = f
            kk_sc[gi] = kk
            d = jnp.exp2(_dot_sel(suffix, jnp.log2(f)))
            kd = kk * d
            fblk = [f[i * SUB:i * SUB + 1, :] * d[i * SUB:i * SUB + 1, :] for i in range(NSUB)]

            qp = []
            for i in range(NSUB):
                qp_heads = []
                for h in range(HG_HEADS):
                    hs = slice(h * HG_DK, (h + 1) * HG_DK)
                    q_lo = qq[i * SUB:i * SUB + HALF, hs]
                    q_hi = qq[i * SUB + HALF:(i + 1) * SUB, hs]
                    w_lo, w_hi = q_lo, q_hi
                    for s in range(SUB - 1, -1, -1):
                        r = i * SUB + s
                        k_s = jnp.broadcast_to(kk_sc[gi, r:r + 1, hs], (HALF, HG_DK))
                        f_s = jnp.broadcast_to(f_sc[gi, r:r + 1, hs], (HALF, HG_DK))
                        lo = w_lo * k_s if s < HALF else zero_half
                        piece = jnp.concatenate([lo, w_hi * k_s], axis=0).astype(BF16)
                        p_sc[gi, h * CHUNK + i * SUB:h * CHUNK + (i + 1) * SUB,
                             s * HG_DK:(s + 1) * HG_DK] = piece
                        w_hi = w_hi * f_s
                        if s > HALF:
                            w_hi = jnp.where(tloc_hi >= s, w_hi, q_hi)
                        if s < HALF:
                            w_lo = w_lo * f_s
                            if s > 0:
                                w_lo = jnp.where(tloc_lo >= s, w_lo, q_lo)
                    qp_heads.append(jnp.concatenate([w_lo, w_hi], axis=0))
                qp.append(jnp.concatenate(qp_heads, axis=1))
            a_diag = _dot(p_sc[gi], sumsel)

            f01 = fblk[0] * fblk[1]
            f12 = fblk[1] * fblk[2]
            f23 = fblk[2] * fblk[3]
            f012 = f01 * fblk[2]
            f123 = fblk[1] * f23
            f_all = f01 * f23
            q_abs = jnp.concatenate(
                [qp[0], qp[1] * fblk[0], qp[2] * f01, qp[3] * f012], axis=0).astype(BF16)
            k_end = jnp.concatenate(
                [blk(kd, 0) * f123, blk(kd, 1) * f23, blk(kd, 2) * fblk[3], blk(kd, 3)],
                axis=0).astype(BF16)
            qpb = [x.astype(BF16) for x in qp]
            kdb = [blk(kd, j).astype(BF16) for j in range(NSUB)]
            kd0_f1 = (blk(kd, 0) * fblk[1]).astype(BF16)
            kd0_f12 = (blk(kd, 0) * f12).astype(BF16)
            kd1_f2 = (blk(kd, 1) * fblk[2]).astype(BF16)
            vb = v_ref[gi, rows, :].astype(BF16)

            for h in range(HG_HEADS):
                hs = slice(h * HG_DK, (h + 1) * HG_DK)
                z = zero_blk
                lhs = jnp.concatenate([
                    jnp.concatenate([z, qpb[1][:, hs], z, z], axis=0),
                    jnp.concatenate([z, z, qpb[2][:, hs], z], axis=0),
                    jnp.concatenate([z, z, z, qpb[3][:, hs]], axis=0)], axis=1)
                rhs = jnp.concatenate([
                    jnp.concatenate([kdb[0][:, hs], z, z, z], axis=0),
                    jnp.concatenate([kd0_f1[:, hs], kdb[1][:, hs], z, z], axis=0),
                    jnp.concatenate([kd0_f12[:, hs], kd1_f2[:, hs], kdb[2][:, hs], z], axis=0)],
                    axis=1)
                a = _dot_nt(lhs, rhs) + jnp.where(
                    diag_mask, a_diag[h * CHUNK:(h + 1) * CHUNK, :], 0.0)
                st = st_ref[gi, h]
                o = _dot(a.astype(BF16), vb[:, hs]) + _dot_nt(q_abs[:, hs], st.astype(BF16))
                st_ref[gi, h] = st * f_all[:, hs] + _dot_tn(vb[:, hs], k_end[:, hs])
                o = o * lax.rsqrt(jnp.mean(o * o, axis=-1, keepdims=True) + RMS_EPS) * ng
                o = o * _silu(g_ref[gi, rows, hs])
                o_ref[gi, rows, hs] = o.astype(o_ref.dtype)
        return carry

    lax.fori_loop(0, SEQ_TILE // CHUNK, chunk_body, 0, unroll=2)


def _seq_spec(group, width):
    return pl.BlockSpec((group, SEQ_TILE, width), lambda b, t: (b, t, 0))


def _hgrn_call(lb_logits, norm_g, hq, hf, hi, hg):
    bsz, seqlen, _ = hq.shape
    grp = HGRN_GROUP
    return pl.pallas_call(
        _hgrn_kernel,
        grid=(bsz // grp, seqlen // SEQ_TILE),
        in_specs=[_resident(lb_logits.shape), _resident(norm_g.shape),
                  _seq_spec(grp, HG_FDIM), _seq_spec(grp, HG_FDIM),
                  _seq_spec(grp, HG_WIDTH), _seq_spec(grp, HG_WIDTH)],
        out_specs=_seq_spec(grp, HG_WIDTH),
        out_shape=jax.ShapeDtypeStruct((bsz, seqlen, HG_WIDTH), BF16),
        scratch_shapes=[
            pltpu.VMEM((grp, HG_HEADS, HG_DV, HG_DK), F32),
            pltpu.VMEM((grp, CHUNK, HG_FDIM), F32),
            pltpu.VMEM((grp, CHUNK, HG_FDIM), F32),
            pltpu.VMEM((grp, HG_HEADS * CHUNK, SUB * HG_DK), BF16),
        ],
        compiler_params=pltpu.CompilerParams(
            dimension_semantics=("arbitrary", "arbitrary"), vmem_limit_bytes=V7X_VMEM_LIMIT_BYTES),
        name="hgrn2_mixer",
    )(lb_logits, norm_g, hq, hf, hi, hg)


def _ssd_kernel(cw_ref, cb_ref, dtb_ref, alog_ref, dexp_ref, ng_ref,
                z_ref, xbc_ref, dt_ref, o_ref, ht_ref, xbuf, xc_sc):
    @pl.when(pl.program_id(1) == 0)
    def _():
        ht_ref[...] = jnp.zeros_like(ht_ref)
        xbuf[:, 0:SUBLANES, :] = jnp.zeros((SSD_GROUP, SUBLANES, SSD_CONV_CH), F32)

    def expander(width, per_head):
        hrow = lax.broadcasted_iota(jnp.int32, (DT_PAD, width), 0)
        hcol = lax.broadcasted_iota(jnp.int32, (DT_PAD, width), 1)
        return (hrow == hcol // per_head).astype(BF16)

    expand_ch = expander(SSD_WIDTH, SSD_HEADDIM)
    expand_tile = expander(SSD_HEADS * SSD_Q, SSD_Q)
    row = lax.broadcasted_iota(jnp.int32, (SSD_Q, SSD_Q), 0)
    col = lax.broadcasted_iota(jnp.int32, (SSD_Q, SSD_Q), 1)
    causal = col <= row
    tri = causal.astype(BF16)
    upper = (row <= col).astype(BF16)
    dexp = dexp_ref[...]
    ng = ng_ref[...]
    rate = -jnp.exp(alog_ref[...]) * LOG2_E

    for gi in range(SSD_GROUP):
        xbuf[gi, SUBLANES:SUBLANES + SEQ_TILE, :] = xbc_ref[gi]
        conv = cb_ref[...]
        for j in range(SSD_CONV):
            off = SUBLANES - (SSD_CONV - 1) + j
            conv = conv + cw_ref[j:j + 1, :] * xbuf[gi, off:off + SEQ_TILE, :]
        xc_sc[gi] = _silu(conv)
        xbuf[gi, 0:SUBLANES, :] = xbuf[gi, SEQ_TILE:SEQ_TILE + SUBLANES, :]

    for c in range(SEQ_TILE // SSD_Q):
        rows = slice(c * SSD_Q, (c + 1) * SSD_Q)
        for gi in range(SSD_GROUP):
            dt = _softplus(dt_ref[gi, rows, :] + dtb_ref[...])
            a = dt * rate
            acum_h = _dot_sel(tri, a)
            acum = _dot_x_sel(acum_h, expand_ch)
            acum_rep = _dot_x_sel(acum_h, expand_tile)
            acum_t = _dot_x_sel(a.T, upper)
            xs = xc_sc[gi, rows, 0:SSD_WIDTH]
            dtx = xs * _dot_x_sel(dt, expand_ch)
            a_last = acum[SSD_Q - 1:SSD_Q, :]
            dtx_end = (dtx * jnp.exp2(a_last - acum)).astype(BF16)
            dtxb = dtx.astype(BF16)
            e_cum = jnp.exp2(acum)
            e_last = jnp.exp2(a_last)

            ys = []
            for g in range(SSD_GROUPS):
                gs = slice(g * SSD_GROUP_W, (g + 1) * SSD_GROUP_W)
                b0 = SSD_WIDTH + g * SSD_STATE
                c0 = SSD_WIDTH + SSD_GROUPS * SSD_STATE + g * SSD_STATE
                bm = xc_sc[gi, rows, b0:b0 + SSD_STATE].astype(BF16)
                cm = xc_sc[gi, rows, c0:c0 + SSD_STATE].astype(BF16)
                cbm = _dot_nt(cm, bm)
                ht = ht_ref[gi, g]
                y_off = _dot(cm, ht.astype(BF16)) * e_cum[:, gs]
                y_diag = []
                for r in range(SSD_HPG):
                    h = g * SSD_HPG + r
                    seg = acum_rep[:, h * SSD_Q:(h + 1) * SSD_Q] - acum_t[h:h + 1, :]
                    lmat = jnp.exp2(jnp.where(causal, seg, NEG_BIG))
                    y_diag.append(_dot((cbm * lmat).astype(BF16),
                                       dtxb[:, h * SSD_HEADDIM:(h + 1) * SSD_HEADDIM]))
                ys.append(jnp.concatenate(y_diag, axis=1) + y_off)
                ht_ref[gi, g] = ht * e_last[:, gs] + _dot_tn(bm, dtx_end[:, gs])
            y = jnp.concatenate(ys, axis=1) + dexp * xs
            y = y * _silu(z_ref[gi, rows, :])
            y = y * lax.rsqrt(jnp.mean(y * y, axis=-1, keepdims=True) + RMS_EPS) * ng
            o_ref[gi, rows, :] = y.astype(o_ref.dtype)


def _ssd_call(conv_w, conv_b, dt_bias, a_log, d_exp, norm_g, sz, sxbc, sdt):
    bsz, seqlen, _ = sz.shape
    grp = SSD_GROUP
    params = (conv_w, conv_b, dt_bias, a_log, d_exp, norm_g)
    return pl.pallas_call(
        _ssd_kernel,
        grid=(bsz // grp, seqlen // SEQ_TILE),
        in_specs=[_resident(p.shape) for p in params]
        + [_seq_spec(grp, SSD_WIDTH), _seq_spec(grp, SSD_CONV_CH), _seq_spec(grp, DT_PAD)],
        out_specs=_seq_spec(grp, SSD_WIDTH),
        out_shape=jax.ShapeDtypeStruct((bsz, seqlen, SSD_WIDTH), BF16),
        scratch_shapes=[
            pltpu.VMEM((grp, SSD_GROUPS, SSD_STATE, SSD_GROUP_W), F32),
            pltpu.VMEM((grp, SEQ_TILE + SUBLANES, SSD_CONV_CH), F32),
            pltpu.VMEM((grp, SEQ_TILE, SSD_CONV_CH), F32),
        ],
        compiler_params=pltpu.CompilerParams(
            dimension_semantics=("arbitrary", "arbitrary"), vmem_limit_bytes=V7X_VMEM_LIMIT_BYTES),
        name="ssd_mixer",
    )(*params, sz, sxbc, sdt)


def _post_kernel(oh_ref, os_ref, x1_ref, p_ref, wo_ref, g2_ref, b2_ref,
                 w2i_ref, w2o_ref, g3_ref, b3_ref, wg_ref, wp_ref, g4_ref, b4_ref, out_ref):
    mix = _dot(oh_ref[...], wo_ref[0:HG_WIDTH, :]) + _dot(os_ref[...], wo_ref[HG_WIDTH:, :])
    x2 = _layer_norm(DEEPNORM_ALPHA * x1_ref[...] + mix, g2_ref[...], b2_ref[...])
    x3 = _ffn_ln(x2, w2i_ref, w2o_ref, g3_ref[...], b3_ref[...])
    gate = _sigmoid(_dot(x3.astype(BF16), wg_ref[...]))
    ple = gate * _dot(p_ref[...].astype(BF16), wp_ref[...])
    out_ref[...] = _layer_norm(DEEPNORM_ALPHA * x3 + ple, g4_ref[...], b4_ref[...])


def _post_call(oh, osd, x1, p2d, wo, g2, b2, w2i, w2o, g3, b3, wg, wp, g4, b4):
    n = x1.shape[0]
    weights = (wo, g2, b2, w2i, w2o, g3, b3, wg, wp, g4, b4)
    return pl.pallas_call(
        _post_kernel,
        grid=(n // TOKEN_TILE,),
        in_specs=[_rows(HG_WIDTH), _rows(SSD_WIDTH), _rows(D_MODEL), _rows(PLE_DIM)]
        + [_resident(w.shape) for w in weights],
        out_specs=_rows(D_MODEL),
        out_shape=jax.ShapeDtypeStruct((n, D_MODEL), F32),
        compiler_params=pltpu.CompilerParams(
            dimension_semantics=("arbitrary",), vmem_limit_bytes=V7X_VMEM_LIMIT_BYTES),
        name="post_outproj_ffn_ple",
    )(oh, osd, x1, p2d, *weights)


def _row(v):
    return v.reshape(1, -1).astype(F32)


def kernel(x, p, ffn1_w_in, ffn1_w_out, ln1_g, ln1_b, w_in_mix, hgrn_lb_logits, hgrn_norm_g,
           ssd_conv_w, ssd_conv_b, ssd_dt_bias, ssd_a_log, ssd_d, ssd_norm_g, w_out_mix,
           ln2_g, ln2_b, ffn2_w_in, ffn2_w_out, ln3_g, ln3_b, ple_w_proj, ple_w_gate,
           ln4_g, ln4_b):
    bsz, seqlen, _ = x.shape
    n = bsz * seqlen
    assert DEPTH == 1 and n % TOKEN_TILE == 0 and seqlen % SEQ_TILE == 0
    assert bsz % HGRN_GROUP == 0 and bsz % SSD_GROUP == 0
    h = x.reshape(n, D_MODEL)
    for i in range(DEPTH):
        pad = MIX_COLS_PADDED - w_in_mix.shape[-1]
        wmix = jnp.pad(w_in_mix[i], ((0, 0), (0, pad))).astype(BF16)
        x1, hq, hf, hi, hg, sz, sxbc, sdt = _pre_call(
            h, ffn1_w_in[i].astype(BF16), ffn1_w_out[i].astype(BF16),
            _row(ln1_g[i]), _row(ln1_b[i]), wmix)

        def seq(t):
            return t.reshape(bsz, seqlen, t.shape[-1])

        o_h = _hgrn_call(hgrn_lb_logits.astype(F32), _row(hgrn_norm_g[i]),
                         seq(hq), seq(hf), seq(hi), seq(hg))
        head_pad = DT_PAD - SSD_HEADS
        o_s = _ssd_call(
            ssd_conv_w[i].astype(F32), _row(ssd_conv_b[i]),
            jnp.pad(_row(ssd_dt_bias[i]), ((0, 0), (0, head_pad))),
            jnp.pad(_row(ssd_a_log[i]), ((0, 0), (0, head_pad))),
            _row(jnp.repeat(ssd_d[i], SSD_HEADDIM)), _row(ssd_norm_g[i]),
            seq(sz), seq(sxbc), seq(sdt))
        h = _post_call(
            o_h.reshape(n, HG_WIDTH), o_s.reshape(n, SSD_WIDTH), x1, p[i].reshape(n, PLE_DIM),
            w_out_mix[i].astype(BF16), _row(ln2_g[i]), _row(ln2_b[i]),
            ffn2_w_in[i].astype(BF16), ffn2_w_out[i].astype(BF16), _row(ln3_g[i]), _row(ln3_b[i]),
            ple_w_gate[i].astype(BF16), ple_w_proj[i].astype(BF16), _row(ln4_g[i]), _row(ln4_b[i]))
    return h.reshape(bsz, seqlen, D_MODEL)
```

```python
import jax
import jax.numpy as jnp
from jax import lax
from jax.experimental import pallas as pl
from jax.experimental.pallas import tpu as pltpu

F32 = jnp.float32
BF16 = jnp.bfloat16

DEPTH = 1
D_MODEL = 1024
D_FF = 2816
CHUNK = 64
PLE_DIM = 256
HG_HEADS = 4
HG_DK = 128
HG_DV = 128
HG_WIDTH = HG_HEADS * HG_DV
HG_FDIM = HG_HEADS * HG_DK
SSD_WIDTH = 512
SSD_HEADDIM = 64
SSD_HEADS = SSD_WIDTH // SSD_HEADDIM
SSD_GROUPS = 2
SSD_HPG = SSD_HEADS // SSD_GROUPS
SSD_STATE = 128
SSD_CONV = 4
SSD_CONV_CH = SSD_WIDTH + 2 * SSD_GROUPS * SSD_STATE
SSD_GROUP_W = SSD_HPG * SSD_HEADDIM
DEEPNORM_ALPHA = (2.0 * DEPTH) ** 0.25
LN_EPS = 1e-5
RMS_EPS = 1e-6
LOG2_E = 1.4426950408889634

LANES = 128
SUBLANES = 8
V7X_VMEM_LIMIT_BYTES = 56 * 1024 * 1024

DT_PAD = LANES
MIX_COLS_PADDED = 2 * HG_FDIM + 2 * HG_WIDTH + SSD_WIDTH + SSD_CONV_CH + DT_PAD
TOKEN_TILE = 256
FF_CHUNK = D_FF // 2
SEQ_TILE = 256
HGRN_TILE = 512
SUB = 16
NSUB = CHUNK // SUB
HALF = SUB // 2
NEG_BIG = -1e30
SSD_Q = 128


def _sigmoid(x):
    return 1.0 / (1.0 + jnp.exp(-x))


def _silu(x):
    return x * _sigmoid(x)


def _softplus(x):
    return jnp.maximum(x, 0.0) + jnp.log1p(jnp.exp(-jnp.abs(x)))


def _dot(a, b):
    return jnp.dot(a, b, preferred_element_type=F32)


def _dot_nt(a, b):
    return lax.dot_general(a, b, (((1,), (1,)), ((), ())), preferred_element_type=F32)


def _dot_tn(a, b):
    return lax.dot_general(a, b, (((0,), (0,)), ((), ())), preferred_element_type=F32)


def _split3(x):
    hi = x.astype(BF16)
    r1 = x - hi.astype(F32)
    mid = r1.astype(BF16)
    lo = (r1 - mid.astype(F32)).astype(BF16)
    return hi, mid, lo


def _dot_sel(sel, x):
    hi, mid, lo = _split3(x)
    return _dot(sel, hi) + _dot(sel, mid) + _dot(sel, lo)


def _dot_x_sel(x, sel):
    hi, mid, lo = _split3(x)
    return _dot(hi, sel) + _dot(mid, sel) + _dot(lo, sel)


def _layer_norm(y, g, b):
    mu = jnp.mean(y, axis=-1, keepdims=True)
    d = y - mu
    var = jnp.mean(d * d, axis=-1, keepdims=True)
    return d * lax.rsqrt(var + LN_EPS) * g + b


def _ffn_ln(x, w_in_ref, w_out_ref, g, b):
    xb = x.astype(BF16)
    acc = None
    for c0 in range(0, D_FF, FF_CHUNK):
        gate = _dot(xb, w_in_ref[:, c0:c0 + FF_CHUNK])
        up = _dot(xb, w_in_ref[:, D_FF + c0:D_FF + c0 + FF_CHUNK])
        act = (_silu(gate) * up).astype(BF16)
        part = _dot(act, w_out_ref[c0:c0 + FF_CHUNK, :])
        acc = part if acc is None else acc + part
    return _layer_norm(DEEPNORM_ALPHA * x + 0.5 * acc, g, b)


def _pre_kernel(x_ref, w1i_ref, w1o_ref, g1_ref, b1_ref, wmix_ref,
                x1_ref, hq_ref, hf_ref, hi_ref, hg_ref, sz_ref, sxbc_ref, sdt_ref):
    x1 = _ffn_ln(x_ref[...], w1i_ref, w1o_ref, g1_ref[...], b1_ref[...])
    x1_ref[...] = x1
    u = _dot(x1.astype(BF16), wmix_ref[...])
    c = 0
    for ref in (hq_ref, hf_ref, hi_ref, hg_ref, sz_ref, sxbc_ref, sdt_ref):
        w = ref.shape[-1]
        ref[...] = u[:, c:c + w]
        c += w


def _resident(shape):
    return pl.BlockSpec(shape, lambda *_: (0,) * len(shape), pipeline_mode=pl.Buffered(1))


def _rows(width):
    return pl.BlockSpec((TOKEN_TILE, width), lambda i: (i, 0))


def _pre_call(x2d, w1i, w1o, g1, b1, wmix):
    n = x2d.shape[0]
    widths = (HG_FDIM, HG_FDIM, HG_WIDTH, HG_WIDTH, SSD_WIDTH, SSD_CONV_CH, DT_PAD)
    out_shape = [jax.ShapeDtypeStruct((n, D_MODEL), F32)] + [
        jax.ShapeDtypeStruct((n, w), F32) for w in widths]
    return pl.pallas_call(
        _pre_kernel,
        grid=(n // TOKEN_TILE,),
        in_specs=[_rows(D_MODEL), _resident(w1i.shape), _resident(w1o.shape),
                  _resident(g1.shape), _resident(b1.shape), _resident(wmix.shape)],
        out_specs=[_rows(D_MODEL)] + [_rows(w) for w in widths],
        out_shape=out_shape,
        compiler_params=pltpu.CompilerParams(
            dimension_semantics=("arbitrary",), vmem_limit_bytes=V7X_VMEM_LIMIT_BYTES),
        name="pre_ffn_inproj",
    )(x2d, w1i, w1o, g1, b1, wmix)


def _hgrn_kernel(lbl_ref, ng_ref, suffix_ref, sumsel_ref, q_ref, f_ref, v_ref, g_ref, o_ref,
                 st_ref, f_sc, kk_sc, p_sc, a_sc, qa_sc, ke_sc, fall_sc):
    @pl.when(pl.program_id(1) == 0)
    def _():
        st_ref[...] = jnp.zeros_like(st_ref)

    logits = lbl_ref[...]
    ex = jnp.exp(logits - jnp.max(logits, axis=0, keepdims=True))
    lb = ex[0:1, :] / jnp.sum(ex, axis=0, keepdims=True)
    ng = ng_ref[...]

    row = lax.broadcasted_iota(jnp.int32, (CHUNK, CHUNK), 0)
    col = lax.broadcasted_iota(jnp.int32, (CHUNK, CHUNK), 1)
    diag_mask = (col // SUB == row // SUB) & (col <= row)
    tloc_lo = lax.broadcasted_iota(jnp.int32, (HALF, HG_DK), 0)
    tloc_hi = tloc_lo + HALF
    zero_half = jnp.zeros((HALF, HG_DK), F32)
    zero_blk = jnp.zeros((SUB, HG_DK), BF16)

    def blk(x, i):
        return x[i * SUB:(i + 1) * SUB, :]

    def scores(c):
        rows = pl.ds(pl.multiple_of(c * CHUNK, CHUNK), CHUNK)
        f = lb + (1.0 - lb) * _sigmoid(f_ref[0, rows, :])
        kk = 1.0 - f
        qq = _silu(q_ref[0, rows, :])
        f_sc[...] = f
        kk_sc[...] = kk
        d = jnp.exp2(_dot_sel(suffix_ref[...], jnp.log2(f)))
        kd = kk * d
        fblk = [f[i * SUB:i * SUB + 1, :] * d[i * SUB:i * SUB + 1, :] for i in range(NSUB)]

        qp = []
        for i in range(NSUB):
            hsl = [slice(h * HG_DK, (h + 1) * HG_DK) for h in range(HG_HEADS)]
            q_lo = [qq[i * SUB:i * SUB + HALF, hs] for hs in hsl]
            q_hi = [qq[i * SUB + HALF:(i + 1) * SUB, hs] for hs in hsl]
            w_lo, w_hi = list(q_lo), list(q_hi)
            for s in range(SUB - 1, -1, -1):
                r = i * SUB + s
                keep_hi = tloc_hi >= s
                keep_lo = tloc_lo >= s
                for h, hs in enumerate(hsl):
                    k_s = jnp.broadcast_to(kk_sc[r:r + 1, hs], (HALF, HG_DK))
                    f_s = jnp.broadcast_to(f_sc[r:r + 1, hs], (HALF, HG_DK))
                    lo = w_lo[h] * k_s if s < HALF else zero_half
                    piece = jnp.concatenate([lo, w_hi[h] * k_s], axis=0).astype(BF16)
                    p_sc[h * CHUNK + i * SUB:h * CHUNK + (i + 1) * SUB,
                         s * HG_DK:(s + 1) * HG_DK] = piece
                    w_hi[h] = w_hi[h] * f_s
                    if s > HALF:
                        w_hi[h] = jnp.where(keep_hi, w_hi[h], q_hi[h])
                    if s < HALF:
                        w_lo[h] = w_lo[h] * f_s
                        if s > 0:
                            w_lo[h] = jnp.where(keep_lo, w_lo[h], q_lo[h])
            qp.append(jnp.concatenate(
                [jnp.concatenate([w_lo[h], w_hi[h]], axis=0) for h in range(HG_HEADS)], axis=1))
        a_diag = _dot(p_sc[...], sumsel_ref[...])

        f01 = fblk[0] * fblk[1]
        f12 = fblk[1] * fblk[2]
        f23 = fblk[2] * fblk[3]
        f012 = f01 * fblk[2]
        f123 = fblk[1] * f23
        fall_sc[0:1, :] = f01 * f23
        qa_sc[...] = jnp.concatenate(
            [qp[0], qp[1] * fblk[0], qp[2] * f01, qp[3] * f012], axis=0).astype(BF16)
        ke_sc[...] = jnp.concatenate(
            [blk(kd, 0) * f123, blk(kd, 1) * f23, blk(kd, 2) * fblk[3], blk(kd, 3)],
            axis=0).astype(BF16)
        qpb = [x.astype(BF16) for x in qp]
        kdb = [blk(kd, j).astype(BF16) for j in range(NSUB)]
        kd0_f1 = (blk(kd, 0) * fblk[1]).astype(BF16)
        kd0_f12 = (blk(kd, 0) * f12).astype(BF16)
        kd1_f2 = (blk(kd, 1) * fblk[2]).astype(BF16)
        for h in range(HG_HEADS):
            hs = slice(h * HG_DK, (h + 1) * HG_DK)
            z = zero_blk
            lhs = jnp.concatenate([
                jnp.concatenate([z, qpb[1][:, hs], z, z], axis=0),
                jnp.concatenate([z, z, qpb[2][:, hs], z], axis=0),
                jnp.concatenate([z, z, z, qpb[3][:, hs]], axis=0)], axis=1)
            rhs = jnp.concatenate([
                jnp.concatenate([kdb[0][:, hs], z, z, z], axis=0),
                jnp.concatenate([kd0_f1[:, hs], kdb[1][:, hs], z, z], axis=0),
                jnp.concatenate([kd0_f12[:, hs], kd1_f2[:, hs], kdb[2][:, hs], z], axis=0)],
                axis=1)
            a = _dot_nt(lhs, rhs) + jnp.where(
                diag_mask, a_diag[h * CHUNK:(h + 1) * CHUNK, :], 0.0)
            a_sc[h] = a.astype(BF16)

    def apply(c):
        rows = pl.ds(pl.multiple_of(c * CHUNK, CHUNK), CHUNK)
        f_all = fall_sc[0:1, :]
        for h in range(HG_HEADS):
            hs = slice(h * HG_DK, (h + 1) * HG_DK)
            vb = v_ref[0, rows, hs].astype(BF16)
            st = st_ref[h]
            o = _dot(a_sc[h], vb) + _dot_nt(qa_sc[:, hs], st.astype(BF16))
            st_ref[h] = st * f_all[:, hs] + _dot_tn(vb, ke_sc[:, hs])
            o = o * lax.rsqrt(jnp.mean(o * o, axis=-1, keepdims=True) + RMS_EPS) * ng
            o = o * _silu(g_ref[0, rows, hs])
            o_ref[0, rows, hs] = o.astype(o_ref.dtype)

    n_chunks = HGRN_TILE // CHUNK
    scores(0)

    def body(c, carry):
        apply(c)
        scores(c + 1)
        return carry

    lax.fori_loop(0, n_chunks - 1, body, 0)
    apply(n_chunks - 1)


def _seq_spec(tile, width):
    return pl.BlockSpec((1, tile, width), lambda b, t: (b, t, 0))


def _hgrn_selectors():
    row = lax.broadcasted_iota(jnp.int32, (CHUNK, CHUNK), 0)
    col = lax.broadcasted_iota(jnp.int32, (CHUNK, CHUNK), 1)
    suffix = ((col // SUB == row // SUB) & (col > row)).astype(BF16)
    srow = lax.broadcasted_iota(jnp.int32, (SUB * HG_DK, CHUNK), 0)
    scol = lax.broadcasted_iota(jnp.int32, (SUB * HG_DK, CHUNK), 1)
    sumsel = (srow // HG_DK == scol % SUB).astype(BF16)
    return suffix, sumsel


def _hgrn_call(lb_logits, norm_g, hq, hf, hi, hg):
    bsz, seqlen, _ = hq.shape
    suffix, sumsel = _hgrn_selectors()
    return pl.pallas_call(
        _hgrn_kernel,
        grid=(bsz, seqlen // HGRN_TILE),
        in_specs=[_resident(lb_logits.shape), _resident(norm_g.shape),
                  _resident(suffix.shape), _resident(sumsel.shape),
                  _seq_spec(HGRN_TILE, HG_FDIM), _seq_spec(HGRN_TILE, HG_FDIM),
                  _seq_spec(HGRN_TILE, HG_WIDTH), _seq_spec(HGRN_TILE, HG_WIDTH)],
        out_specs=_seq_spec(HGRN_TILE, HG_WIDTH),
        out_shape=jax.ShapeDtypeStruct((bsz, seqlen, HG_WIDTH), BF16),
        scratch_shapes=[
            pltpu.VMEM((HG_HEADS, HG_DV, HG_DK), F32),
            pltpu.VMEM((CHUNK, HG_FDIM), F32),
            pltpu.VMEM((CHUNK, HG_FDIM), F32),
            pltpu.VMEM((HG_HEADS * CHUNK, SUB * HG_DK), BF16),
            pltpu.VMEM((HG_HEADS, CHUNK, CHUNK), BF16),
            pltpu.VMEM((CHUNK, HG_FDIM), BF16),
            pltpu.VMEM((CHUNK, HG_FDIM), BF16),
            pltpu.VMEM((SUBLANES, HG_FDIM), F32),
        ],
        compiler_params=pltpu.CompilerParams(
            dimension_semantics=("arbitrary", "arbitrary"), vmem_limit_bytes=V7X_VMEM_LIMIT_BYTES),
        name="hgrn2_mixer",
    )(lb_logits, norm_g, suffix, sumsel, hq, hf, hi, hg)


def _ssd_kernel(cw_ref, cb_ref, dtb_ref, alog_ref, dexp_ref, ng_ref,
                z_ref, xbc_ref, dt_ref, o_ref, ht_ref, xbuf, xc_sc):
    @pl.when(pl.program_id(1) == 0)
    def _():
        ht_ref[...] = jnp.zeros_like(ht_ref)
        xbuf[0:SUBLANES, :] = jnp.zeros((SUBLANES, SSD_CONV_CH), F32)

    def expander(width, per_head):
        hrow = lax.broadcasted_iota(jnp.int32, (DT_PAD, width), 0)
        hcol = lax.broadcasted_iota(jnp.int32, (DT_PAD, width), 1)
        return (hrow == hcol // per_head).astype(BF16)

    expand_ch = expander(SSD_WIDTH, SSD_HEADDIM)
    expand_tile = expander(SSD_HEADS * SSD_Q, SSD_Q)
    row = lax.broadcasted_iota(jnp.int32, (SSD_Q, SSD_Q), 0)
    col = lax.broadcasted_iota(jnp.int32, (SSD_Q, SSD_Q), 1)
    causal = col <= row
    tri = causal.astype(BF16)
    upper = (row <= col).astype(BF16)
    dexp = dexp_ref[...]
    ng = ng_ref[...]
    rate = -jnp.exp(alog_ref[...]) * LOG2_E

    xbuf[SUBLANES:SUBLANES + SEQ_TILE, :] = xbc_ref[0]
    conv = cb_ref[...]
    for j in range(SSD_CONV):
        off = SUBLANES - (SSD_CONV - 1) + j
        conv = conv + cw_ref[j:j + 1, :] * xbuf[off:off + SEQ_TILE, :]
    xc_sc[...] = _silu(conv)
    xbuf[0:SUBLANES, :] = xbuf[SEQ_TILE:SEQ_TILE + SUBLANES, :]

    for c in range(SEQ_TILE // SSD_Q):
        rows = slice(c * SSD_Q, (c + 1) * SSD_Q)
        dt = _softplus(dt_ref[0, rows, :] + dtb_ref[...])
        a = dt * rate
        acum_h = _dot_sel(tri, a)
        acum = _dot_x_sel(acum_h, expand_ch)
        acum_rep = _dot_x_sel(acum_h, expand_tile)
        acum_t = _dot_x_sel(a.T, upper)
        xs = xc_sc[rows, 0:SSD_WIDTH]
        dtx = xs * _dot_x_sel(dt, expand_ch)
        a_last = acum[SSD_Q - 1:SSD_Q, :]
        dtx_end = (dtx * jnp.exp2(a_last - acum)).astype(BF16)
        dtxb = dtx.astype(BF16)
        e_cum = jnp.exp2(acum)
        e_last = jnp.exp2(a_last)

        ys = []
        for g in range(SSD_GROUPS):
            gs = slice(g * SSD_GROUP_W, (g + 1) * SSD_GROUP_W)
            b0 = SSD_WIDTH + g * SSD_STATE
            c0 = SSD_WIDTH + SSD_GROUPS * SSD_STATE + g * SSD_STATE
            bm = xc_sc[rows, b0:b0 + SSD_STATE].astype(BF16)
            cm = xc_sc[rows, c0:c0 + SSD_STATE].astype(BF16)
            cbm = _dot_nt(cm, bm)
            ht = ht_ref[g]
            y_off = _dot(cm, ht.astype(BF16)) * e_cum[:, gs]
            y_diag = []
            for r in range(SSD_HPG):
                h = g * SSD_HPG + r
                seg = acum_rep[:, h * SSD_Q:(h + 1) * SSD_Q] - acum_t[h:h + 1, :]
                lmat = jnp.exp2(jnp.where(causal, seg, NEG_BIG))
                y_diag.append(_dot((cbm * lmat).astype(BF16),
                                   dtxb[:, h * SSD_HEADDIM:(h + 1) * SSD_HEADDIM]))
            ys.append(jnp.concatenate(y_diag, axis=1) + y_off)
            ht_ref[g] = ht * e_last[:, gs] + _dot_tn(bm, dtx_end[:, gs])
        y = jnp.concatenate(ys, axis=1) + dexp * xs
        y = y * _silu(z_ref[0, rows, :])
        y = y * lax.rsqrt(jnp.mean(y * y, axis=-1, keepdims=True) + RMS_EPS) * ng
        o_ref[0, rows, :] = y.astype(o_ref.dtype)


def _ssd_call(conv_w, conv_b, dt_bias, a_log, d_exp, norm_g, sz, sxbc, sdt):
    bsz, seqlen, _ = sz.shape
    params = (conv_w, conv_b, dt_bias, a_log, d_exp, norm_g)
    return pl.pallas_call(
        _ssd_kernel,
        grid=(bsz, seqlen // SEQ_TILE),
        in_specs=[_resident(p.shape) for p in params]
        + [_seq_spec(SEQ_TILE, SSD_WIDTH), _seq_spec(SEQ_TILE, SSD_CONV_CH),
           _seq_spec(SEQ_TILE, DT_PAD)],
        out_specs=_seq_spec(SEQ_TILE, SSD_WIDTH),
        out_shape=jax.ShapeDtypeStruct((bsz, seqlen, SSD_WIDTH), BF16),
        scratch_shapes=[
            pltpu.VMEM((SSD_GROUPS, SSD_STATE, SSD_GROUP_W), F32),
            pltpu.VMEM((SEQ_TILE + SUBLANES, SSD_CONV_CH), F32),
            pltpu.VMEM((SEQ_TILE, SSD_CONV_CH), F32),
        ],
        compiler_params=pltpu.CompilerParams(
            dimension_semantics=("arbitrary", "arbitrary"), vmem_limit_bytes=V7X_VMEM_LIMIT_BYTES),
        name="ssd_mixer",
    )(*params, sz, sxbc, sdt)


def _post_kernel(oh_ref, os_ref, x1_ref, p_ref, wo_ref, g2_ref, b2_ref,
                 w2i_ref, w2o_ref, g3_ref, b3_ref, wg_ref, wp_ref, g4_ref, b4_ref, out_ref):
    mix = _dot(oh_ref[...], wo_ref[0:HG_WIDTH, :]) + _dot(os_ref[...], wo_ref[HG_WIDTH:, :])
    x2 = _layer_norm(DEEPNORM_ALPHA * x1_ref[...] + mix, g2_ref[...], b2_ref[...])
    x3 = _ffn_ln(x2, w2i_ref, w2o_ref, g3_ref[...], b3_ref[...])
    gate = _sigmoid(_dot(x3.astype(BF16), wg_ref[...]))
    ple = gate * _dot(p_ref[...].astype(BF16), wp_ref[...])
    out_ref[...] = _layer_norm(DEEPNORM_ALPHA * x3 + ple, g4_ref[...], b4_ref[...])


def _post_call(oh, osd, x1, p2d, wo, g2, b2, w2i, w2o, g3, b3, wg, wp, g4, b4):
    n = x1.shape[0]
    weights = (wo, g2, b2, w2i, w2o, g3, b3, wg, wp, g4, b4)
    return pl.pallas_call(
        _post_kernel,
        grid=(n // TOKEN_TILE,),
        in_specs=[_rows(HG_WIDTH), _rows(SSD_WIDTH), _rows(D_MODEL), _rows(PLE_DIM)]
        + [_resident(w.shape) for w in weights],
        out_specs=_rows(D_MODEL),
        out_shape=jax.ShapeDtypeStruct((n, D_MODEL), F32),
        compiler_params=pltpu.CompilerParams(
            dimension_semantics=("arbitrary",), vmem_limit_bytes=V7X_VMEM_LIMIT_BYTES),
        name="post_outproj_ffn_ple",
    )(oh, osd, x1, p2d, *weights)


def _row(v):
    return v.reshape(1, -1).astype(F32)


def kernel(x, p, ffn1_w_in, ffn1_w_out, ln1_g, ln1_b, w_in_mix, hgrn_lb_logits, hgrn_norm_g,
           ssd_conv_w, ssd_conv_b, ssd_dt_bias, ssd_a_log, ssd_d, ssd_norm_g, w_out_mix,
           ln2_g, ln2_b, ffn2_w_in, ffn2_w_out, ln3_g, ln3_b, ple_w_proj, ple_w_gate,
           ln4_g, ln4_b):
    bsz, seqlen, _ = x.shape
    n = bsz * seqlen
    assert DEPTH == 1 and n % TOKEN_TILE == 0
    assert seqlen % SEQ_TILE == 0 and seqlen % HGRN_TILE == 0
    h = x.reshape(n, D_MODEL)
    for i in range(DEPTH):
        pad = MIX_COLS_PADDED - w_in_mix.shape[-1]
        wmix = jnp.pad(w_in_mix[i], ((0, 0), (0, pad))).astype(BF16)
        x1, hq, hf, hi, hg, sz, sxbc, sdt = _pre_call(
            h, ffn1_w_in[i].astype(BF16), ffn1_w_out[i].astype(BF16),
            _row(ln1_g[i]), _row(ln1_b[i]), wmix)

        def seq(t):
            return t.reshape(bsz, seqlen, t.shape[-1])

        o_h = _hgrn_call(hgrn_lb_logits.astype(F32), _row(hgrn_norm_g[i]),
                         seq(hq), seq(hf), seq(hi), seq(hg))
        head_pad = DT_PAD - SSD_HEADS
        o_s = _ssd_call(
            ssd_conv_w[i].astype(F32), _row(ssd_conv_b[i]),
            jnp.pad(_row(ssd_dt_bias[i]), ((0, 0), (0, head_pad))),
            jnp.pad(_row(ssd_a_log[i]), ((0, 0), (0, head_pad))),
            _row(jnp.repeat(ssd_d[i], SSD_HEADDIM)), _row(ssd_norm_g[i]),
            seq(sz), seq(sxbc), seq(sdt))
        h = _post_call(
            o_h.reshape(n, HG_WIDTH), o_s.reshape(n, SSD_WIDTH), x1, p[i].reshape(n, PLE_DIM),
            w_out_mix[i].astype(BF16), _row(ln2_g[i]), _row(ln2_b[i]),
            ffn2_w_in[i].astype(BF16), ffn2_w_out[i].astype(BF16), _row(ln3_g[i]), _row(ln3_b[i]),
            ple_w_gate[i].astype(BF16), ple_w_proj[i].astype(BF16), _row(ln4_g[i]), _row(ln4_b[i]))
    return h.reshape(bsz, seqlen, D_MODEL)
```

```python
import jax
import jax.numpy as jnp
from jax import lax
from jax.experimental import pallas as pl
from jax.experimental.pallas import tpu as pltpu

F32 = jnp.float32
BF16 = jnp.bfloat16

DEPTH = 1
D_MODEL = 1024
D_FF = 2816
CHUNK = 64
PLE_DIM = 256
HG_HEADS = 4
HG_DK = 128
HG_DV = 128
HG_WIDTH = HG_HEADS * HG_DV
HG_FDIM = HG_HEADS * HG_DK
SSD_WIDTH = 512
SSD_HEADDIM = 64
SSD_HEADS = SSD_WIDTH // SSD_HEADDIM
SSD_GROUPS = 2
SSD_HPG = SSD_HEADS // SSD_GROUPS
SSD_STATE = 128
SSD_CONV = 4
SSD_CONV_CH = SSD_WIDTH + 2 * SSD_GROUPS * SSD_STATE
SSD_GROUP_W = SSD_HPG * SSD_HEADDIM
DEEPNORM_ALPHA = (2.0 * DEPTH) ** 0.25
LN_EPS = 1e-5
RMS_EPS = 1e-6
LOG2_E = 1.4426950408889634

LANES = 128
SUBLANES = 8
V7X_VMEM_LIMIT_BYTES = 56 * 1024 * 1024

DT_PAD = LANES
MIX_COLS_PADDED = 2 * HG_FDIM + 2 * HG_WIDTH + SSD_WIDTH + SSD_CONV_CH + DT_PAD
TOKEN_TILE = 512
SUBTILE = 256
_SUBTILES = tuple(slice(r, r + SUBTILE) for r in range(0, TOKEN_TILE, SUBTILE))
V7X_MXU_DIM = 256
FF_SPLITS = (0, (D_FF // V7X_MXU_DIM + 1) // 2 * V7X_MXU_DIM, D_FF)
SEQ_TILE = 256
HGRN_TILE = 512
SUB = 16
NSUB = CHUNK // SUB
HALF = SUB // 2
NEG_BIG = -1e30
SSD_Q = 128


def _sigmoid(x):
    return 1.0 / (1.0 + jnp.exp(-x))


def _silu(x):
    return x * _sigmoid(x)


def _softplus(x):
    return jnp.maximum(x, 0.0) + jnp.log1p(jnp.exp(-jnp.abs(x)))


def _dot(a, b):
    return jnp.dot(a, b, preferred_element_type=F32)


def _dot_nt(a, b):
    return lax.dot_general(a, b, (((1,), (1,)), ((), ())), preferred_element_type=F32)


def _dot_tn(a, b):
    return lax.dot_general(a, b, (((0,), (0,)), ((), ())), preferred_element_type=F32)


def _split3(x):
    hi = x.astype(BF16)
    r1 = x - hi.astype(F32)
    mid = r1.astype(BF16)
    lo = (r1 - mid.astype(F32)).astype(BF16)
    return hi, mid, lo


def _dot_sel(sel, x):
    hi, mid, lo = _split3(x)
    return _dot(sel, hi) + _dot(sel, mid) + _dot(sel, lo)


def _dot_x_sel(x, sel):
    hi, mid, lo = _split3(x)
    return _dot(hi, sel) + _dot(mid, sel) + _dot(lo, sel)


def _layer_norm(y, g, b):
    mu = jnp.mean(y, axis=-1, keepdims=True)
    d = y - mu
    var = jnp.mean(d * d, axis=-1, keepdims=True)
    return d * lax.rsqrt(var + LN_EPS) * g + b


def _ffn_ln(x, w_in_ref, w_out_ref, g, b):
    xb = x.astype(BF16)
    acc = None
    for c0, c1 in zip(FF_SPLITS[:-1], FF_SPLITS[1:]):
        gate = _dot(xb, w_in_ref[:, c0:c1])
        up = _dot(xb, w_in_ref[:, D_FF + c0:D_FF + c1])
        act = (_silu(gate) * up).astype(BF16)
        part = _dot(act, w_out_ref[c0:c1, :])
        acc = part if acc is None else acc + part
    return _layer_norm(DEEPNORM_ALPHA * x + 0.5 * acc, g, b)


def _pre_kernel(x_ref, w1i_ref, w1o_ref, g1_ref, b1_ref, x1_ref, x1b_ref):
    for rows in _SUBTILES:
        x1 = _ffn_ln(x_ref[rows, :], w1i_ref, w1o_ref, g1_ref[...], b1_ref[...])
        x1_ref[rows, :] = x1
        x1b_ref[rows, :] = x1.astype(BF16)


def _resident(shape):
    return pl.BlockSpec(shape, lambda *_: (0,) * len(shape), pipeline_mode=pl.Buffered(1))


def _rows(width):
    return pl.BlockSpec((TOKEN_TILE, width), lambda i: (i, 0))


def _pre_call(x2d, w1i, w1o, g1, b1):
    n = x2d.shape[0]
    return pl.pallas_call(
        _pre_kernel,
        grid=(n // TOKEN_TILE,),
        in_specs=[_rows(D_MODEL), _resident(w1i.shape), _resident(w1o.shape),
                  _resident(g1.shape), _resident(b1.shape)],
        out_specs=[_rows(D_MODEL), _rows(D_MODEL)],
        out_shape=[jax.ShapeDtypeStruct((n, D_MODEL), F32),
                   jax.ShapeDtypeStruct((n, D_MODEL), BF16)],
        compiler_params=pltpu.CompilerParams(
            dimension_semantics=("arbitrary",), vmem_limit_bytes=V7X_VMEM_LIMIT_BYTES),
        name="pre_ffn",
    )(x2d, w1i, w1o, g1, b1)


_MIX_WIDTHS = (HG_FDIM, HG_FDIM, HG_WIDTH, HG_WIDTH, SSD_WIDTH, SSD_CONV_CH, DT_PAD)
_MIX_DTYPES = (F32, F32, BF16, F32, F32, F32, F32)


def _proj_kernel(x1b_ref, wmix_ref, *out_refs):
    c = 0
    for ref in out_refs:
        w = ref.shape[-1]
        ref[...] = _dot(x1b_ref[...], wmix_ref[:, c:c + w]).astype(ref.dtype)
        c += w


def _proj_call(x1b, wmix):
    n = x1b.shape[0]
    return pl.pallas_call(
        _proj_kernel,
        grid=(n // TOKEN_TILE,),
        in_specs=[_rows(D_MODEL), _resident(wmix.shape)],
        out_specs=[_rows(w) for w in _MIX_WIDTHS],
        out_shape=[jax.ShapeDtypeStruct((n, w), d) for w, d in zip(_MIX_WIDTHS, _MIX_DTYPES)],
        compiler_params=pltpu.CompilerParams(
            dimension_semantics=("arbitrary",), vmem_limit_bytes=V7X_VMEM_LIMIT_BYTES),
        name="mix_inproj",
    )(x1b, wmix)


def _hgrn_kernel(lbl_ref, ng_ref, suffix_ref, sumsel_ref, q_ref, f_ref, v_ref, g_ref, o_ref,
                 st_ref, f_sc, kk_sc, p_sc, a_sc, qa_sc, ke_sc, fall_sc):
    @pl.when(pl.program_id(1) == 0)
    def _():
        st_ref[...] = jnp.zeros_like(st_ref)

    logits = lbl_ref[...]
    ex = jnp.exp(logits - jnp.max(logits, axis=0, keepdims=True))
    lb = ex[0:1, :] / jnp.sum(ex, axis=0, keepdims=True)
    ng = ng_ref[...]

    row = lax.broadcasted_iota(jnp.int32, (CHUNK, CHUNK), 0)
    col = lax.broadcasted_iota(jnp.int32, (CHUNK, CHUNK), 1)
    diag_mask = (col // SUB == row // SUB) & (col <= row)
    tloc_lo = lax.broadcasted_iota(jnp.int32, (HALF, HG_DK), 0)
    tloc_hi = tloc_lo + HALF
    zero_half = jnp.zeros((HALF, HG_DK), F32)
    zero_blk = jnp.zeros((SUB, HG_DK), BF16)

    def blk(x, i):
        return x[i * SUB:(i + 1) * SUB, :]

    def scores(c):
        rows = pl.ds(pl.multiple_of(c * CHUNK, CHUNK), CHUNK)
        f = lb + (1.0 - lb) * _sigmoid(f_ref[0, rows, :])
        kk = 1.0 - f
        qq = _silu(q_ref[0, rows, :])
        f_sc[...] = f
        kk_sc[...] = kk
        d = jnp.exp2(_dot_sel(suffix_ref[...], jnp.log2(f)))
        kd = kk * d
        fblk = [f[i * SUB:i * SUB + 1, :] * d[i * SUB:i * SUB + 1, :] for i in range(NSUB)]

        qp = []
        for i in range(NSUB):
            hsl = [slice(h * HG_DK, (h + 1) * HG_DK) for h in range(HG_HEADS)]
            q_lo = [qq[i * SUB:i * SUB + HALF, hs] for hs in hsl]
            q_hi = [qq[i * SUB + HALF:(i + 1) * SUB, hs] for hs in hsl]
            w_lo, w_hi = list(q_lo), list(q_hi)
            for s in range(SUB - 1, -1, -1):
                r = i * SUB + s
                keep_hi = tloc_hi >= s
                keep_lo = tloc_lo >= s
                for h, hs in enumerate(hsl):
                    k_s = jnp.broadcast_to(kk_sc[r:r + 1, hs], (HALF, HG_DK))
                    f_s = jnp.broadcast_to(f_sc[r:r + 1, hs], (HALF, HG_DK))
                    lo = w_lo[h] * k_s if s < HALF else zero_half
                    piece = jnp.concatenate([lo, w_hi[h] * k_s], axis=0).astype(BF16)
                    p_sc[h * CHUNK + i * SUB:h * CHUNK + (i + 1) * SUB,
                         s * HG_DK:(s + 1) * HG_DK] = piece
                    w_hi[h] = w_hi[h] * f_s
                    if s > HALF:
                        w_hi[h] = jnp.where(keep_hi, w_hi[h], q_hi[h])
                    if s < HALF:
                        w_lo[h] = w_lo[h] * f_s
                        if s > 0:
                            w_lo[h] = jnp.where(keep_lo, w_lo[h], q_lo[h])
            qp.append(jnp.concatenate(
                [jnp.concatenate([w_lo[h], w_hi[h]], axis=0) for h in range(HG_HEADS)], axis=1))
        a_diag = _dot(p_sc[...], sumsel_ref[...])

        f01 = fblk[0] * fblk[1]
        f12 = fblk[1] * fblk[2]
        f23 = fblk[2] * fblk[3]
        f012 = f01 * fblk[2]
        f123 = fblk[1] * f23
        fall_sc[0:1, :] = f01 * f23
        qa_sc[...] = jnp.concatenate(
            [qp[0], qp[1] * fblk[0], qp[2] * f01, qp[3] * f012], axis=0).astype(BF16)
        ke_sc[...] = jnp.concatenate(
            [blk(kd, 0) * f123, blk(kd, 1) * f23, blk(kd, 2) * fblk[3], blk(kd, 3)],
            axis=0).astype(BF16)
        qpb = [x.astype(BF16) for x in qp]
        kdb = [blk(kd, j).astype(BF16) for j in range(NSUB)]
        kd0_f1 = (blk(kd, 0) * fblk[1]).astype(BF16)
        kd0_f12 = (blk(kd, 0) * f12).astype(BF16)
        kd1_f2 = (blk(kd, 1) * fblk[2]).astype(BF16)
        for h in range(HG_HEADS):
            hs = slice(h * HG_DK, (h + 1) * HG_DK)
            z = zero_blk
            lhs = jnp.concatenate([
                jnp.concatenate([z, qpb[1][:, hs], z, z], axis=0),
                jnp.concatenate([z, z, qpb[2][:, hs], z], axis=0),
                jnp.concatenate([z, z, z, qpb[3][:, hs]], axis=0)], axis=1)
            rhs = jnp.concatenate([
                jnp.concatenate([kdb[0][:, hs], z, z, z], axis=0),
                jnp.concatenate([kd0_f1[:, hs], kdb[1][:, hs], z, z], axis=0),
                jnp.concatenate([kd0_f12[:, hs], kd1_f2[:, hs], kdb[2][:, hs], z], axis=0)],
                axis=1)
            a = _dot_nt(lhs, rhs) + jnp.where(
                diag_mask, a_diag[h * CHUNK:(h + 1) * CHUNK, :], 0.0)
            a_sc[h] = a.astype(BF16)

    def apply(c):
        rows = pl.ds(pl.multiple_of(c * CHUNK, CHUNK), CHUNK)
        f_all = fall_sc[0:1, :]
        for h in range(HG_HEADS):
            hs = slice(h * HG_DK, (h + 1) * HG_DK)
            vb = v_ref[0, rows, hs]
            st = st_ref[h]
            o = _dot(a_sc[h], vb) + _dot_nt(qa_sc[:, hs], st.astype(BF16))
            st_ref[h] = st * f_all[:, hs] + _dot_tn(vb, ke_sc[:, hs])
            o = o * lax.rsqrt(jnp.mean(o * o, axis=-1, keepdims=True) + RMS_EPS) * ng
            o = o * _silu(g_ref[0, rows, hs])
            o_ref[0, rows, hs] = o.astype(o_ref.dtype)

    n_chunks = HGRN_TILE // CHUNK
    scores(0)

    def body(c, carry):
        apply(c)
        scores(c + 1)
        return carry

    lax.fori_loop(0, n_chunks - 1, body, 0)
    apply(n_chunks - 1)


def _seq_spec(tile, width):
    return pl.BlockSpec((1, tile, width), lambda b, t: (b, t, 0))


def _hgrn_selectors():
    row = lax.broadcasted_iota(jnp.int32, (CHUNK, CHUNK), 0)
    col = lax.broadcasted_iota(jnp.int32, (CHUNK, CHUNK), 1)
    suffix = ((col // SUB == row // SUB) & (col > row)).astype(BF16)
    srow = lax.broadcasted_iota(jnp.int32, (SUB * HG_DK, CHUNK), 0)
    scol = lax.broadcasted_iota(jnp.int32, (SUB * HG_DK, CHUNK), 1)
    sumsel = (srow // HG_DK == scol % SUB).astype(BF16)
    return suffix, sumsel


def _hgrn_call(lb_logits, norm_g, hq, hf, hi, hg):
    bsz, seqlen, _ = hq.shape
    suffix, sumsel = _hgrn_selectors()
    return pl.pallas_call(
        _hgrn_kernel,
        grid=(bsz, seqlen // HGRN_TILE),
        in_specs=[_resident(lb_logits.shape), _resident(norm_g.shape),
                  _resident(suffix.shape), _resident(sumsel.shape),
                  _seq_spec(HGRN_TILE, HG_FDIM), _seq_spec(HGRN_TILE, HG_FDIM),
                  _seq_spec(HGRN_TILE, HG_WIDTH), _seq_spec(HGRN_TILE, HG_WIDTH)],
        out_specs=_seq_spec(HGRN_TILE, HG_WIDTH),
        out_shape=jax.ShapeDtypeStruct((bsz, seqlen, HG_WIDTH), BF16),
        scratch_shapes=[
            pltpu.VMEM((HG_HEADS, HG_DV, HG_DK), F32),
            pltpu.VMEM((CHUNK, HG_FDIM), F32),
            pltpu.VMEM((CHUNK, HG_FDIM), F32),
            pltpu.VMEM((HG_HEADS * CHUNK, SUB * HG_DK), BF16),
            pltpu.VMEM((HG_HEADS, CHUNK, CHUNK), BF16),
            pltpu.VMEM((CHUNK, HG_FDIM), BF16),
            pltpu.VMEM((CHUNK, HG_FDIM), BF16),
            pltpu.VMEM((SUBLANES, HG_FDIM), F32),
        ],
        compiler_params=pltpu.CompilerParams(
            dimension_semantics=("arbitrary", "arbitrary"), vmem_limit_bytes=V7X_VMEM_LIMIT_BYTES),
        name="hgrn2_mixer",
    )(lb_logits, norm_g, suffix, sumsel, hq, hf, hi, hg)


def _ssd_kernel(cw_ref, cb_ref, dtb_ref, alog_ref, dexp_ref, ng_ref,
                z_ref, xbc_ref, dt_ref, o_ref, ht_ref, xbuf, xc_sc):
    @pl.when(pl.program_id(1) == 0)
    def _():
        ht_ref[...] = jnp.zeros_like(ht_ref)
        xbuf[0:SUBLANES, :] = jnp.zeros((SUBLANES, SSD_CONV_CH), F32)

    def expander(width, per_head):
        hrow = lax.broadcasted_iota(jnp.int32, (DT_PAD, width), 0)
        hcol = lax.broadcasted_iota(jnp.int32, (DT_PAD, width), 1)
        return (hrow == hcol // per_head).astype(BF16)

    expand_ch = expander(SSD_WIDTH, SSD_HEADDIM)
    expand_tile = expander(SSD_HEADS * SSD_Q, SSD_Q)
    row = lax.broadcasted_iota(jnp.int32, (SSD_Q, SSD_Q), 0)
    col = lax.broadcasted_iota(jnp.int32, (SSD_Q, SSD_Q), 1)
    causal = col <= row
    tri = causal.astype(BF16)
    upper = (row <= col).astype(BF16)
    dexp = dexp_ref[...]
    ng = ng_ref[...]
    rate = -jnp.exp(alog_ref[...]) * LOG2_E

    xbuf[SUBLANES:SUBLANES + SEQ_TILE, :] = xbc_ref[0]
    conv = cb_ref[...]
    for j in range(SSD_CONV):
        off = SUBLANES - (SSD_CONV - 1) + j
        conv = conv + cw_ref[j:j + 1, :] * xbuf[off:off + SEQ_TILE, :]
    xc_sc[...] = _silu(conv)
    xbuf[0:SUBLANES, :] = xbuf[SEQ_TILE:SEQ_TILE + SUBLANES, :]

    for c in range(SEQ_TILE // SSD_Q):
        rows = slice(c * SSD_Q, (c + 1) * SSD_Q)
        dt = _softplus(dt_ref[0, rows, :] + dtb_ref[...])
        a = dt * rate
        acum_h = _dot_sel(tri, a)
        acum = _dot_x_sel(acum_h, expand_ch)
        acum_rep = _dot_x_sel(acum_h, expand_tile)
        acum_t = _dot_x_sel(a.T, upper)
        xs = xc_sc[rows, 0:SSD_WIDTH]
        dtx = xs * _dot_x_sel(dt, expand_ch)
        a_last = acum[SSD_Q - 1:SSD_Q, :]
        dtx_end = (dtx * jnp.exp2(a_last - acum)).astype(BF16)
        dtxb = dtx.astype(BF16)
        e_cum = jnp.exp2(acum)
        e_last = jnp.exp2(a_last)

        ys = []
        for g in range(SSD_GROUPS):
            gs = slice(g * SSD_GROUP_W, (g + 1) * SSD_GROUP_W)
            b0 = SSD_WIDTH + g * SSD_STATE
            c0 = SSD_WIDTH + SSD_GROUPS * SSD_STATE + g * SSD_STATE
            bm = xc_sc[rows, b0:b0 + SSD_STATE].astype(BF16)
            cm = xc_sc[rows, c0:c0 + SSD_STATE].astype(BF16)
            cbm = _dot_nt(cm, bm)
            ht = ht_ref[g]
            y_off = _dot(cm, ht.astype(BF16)) * e_cum[:, gs]
            y_diag = []
            for r in range(SSD_HPG):
                h = g * SSD_HPG + r
                seg = acum_rep[:, h * SSD_Q:(h + 1) * SSD_Q] - acum_t[h:h + 1, :]
                lmat = jnp.exp2(jnp.where(causal, seg, NEG_BIG))
                y_diag.append(_dot((cbm * lmat).astype(BF16),
                                   dtxb[:, h * SSD_HEADDIM:(h + 1) * SSD_HEADDIM]))
            ys.append(jnp.concatenate(y_diag, axis=1) + y_off)
            ht_ref[g] = ht * e_last[:, gs] + _dot_tn(bm, dtx_end[:, gs])
        y = jnp.concatenate(ys, axis=1) + dexp * xs
        y = y * _silu(z_ref[0, rows, :])
        y = y * lax.rsqrt(jnp.mean(y * y, axis=-1, keepdims=True) + RMS_EPS) * ng
        o_ref[0, rows, :] = y.astype(o_ref.dtype)


def _ssd_call(conv_w, conv_b, dt_bias, a_log, d_exp, norm_g, sz, sxbc, sdt):
    bsz, seqlen, _ = sz.shape
    params = (conv_w, conv_b, dt_bias, a_log, d_exp, norm_g)
    return pl.pallas_call(
        _ssd_kernel,
        grid=(bsz, seqlen // SEQ_TILE),
        in_specs=[_resident(p.shape) for p in params]
        + [_seq_spec(SEQ_TILE, SSD_WIDTH), _seq_spec(SEQ_TILE, SSD_CONV_CH),
           _seq_spec(SEQ_TILE, DT_PAD)],
        out_specs=_seq_spec(SEQ_TILE, SSD_WIDTH),
        out_shape=jax.ShapeDtypeStruct((bsz, seqlen, SSD_WIDTH), BF16),
        scratch_shapes=[
            pltpu.VMEM((SSD_GROUPS, SSD_STATE, SSD_GROUP_W), F32),
            pltpu.VMEM((SEQ_TILE + SUBLANES, SSD_CONV_CH), F32),
            pltpu.VMEM((SEQ_TILE, SSD_CONV_CH), F32),
        ],
        compiler_params=pltpu.CompilerParams(
            dimension_semantics=("arbitrary", "arbitrary"), vmem_limit_bytes=V7X_VMEM_LIMIT_BYTES),
        name="ssd_mixer",
    )(*params, sz, sxbc, sdt)


def _post_kernel(oh_ref, os_ref, x1_ref, p_ref, wo_ref, g2_ref, b2_ref,
                 w2i_ref, w2o_ref, g3_ref, b3_ref, wg_ref, wp_ref, g4_ref, b4_ref, out_ref):
    x2 = []
    for rows in _SUBTILES:
        mix = (_dot(oh_ref[rows, :], wo_ref[0:HG_WIDTH, :])
               + _dot(os_ref[rows, :], wo_ref[HG_WIDTH:, :]))
        x2.append(_layer_norm(DEEPNORM_ALPHA * x1_ref[rows, :] + mix, g2_ref[...], b2_ref[...]))
    x3 = [_ffn_ln(x, w2i_ref, w2o_ref, g3_ref[...], b3_ref[...]) for x in x2]
    for rows, x in zip(_SUBTILES, x3):
        gate = _sigmoid(_dot(x.astype(BF16), wg_ref[...]))
        ple = gate * _dot(p_ref[rows, :].astype(BF16), wp_ref[...])
        out_ref[rows, :] = _layer_norm(DEEPNORM_ALPHA * x + ple, g4_ref[...], b4_ref[...])


def _post_call(oh, osd, x1, p2d, wo, g2, b2, w2i, w2o, g3, b3, wg, wp, g4, b4):
    n = x1.shape[0]
    weights = (wo, g2, b2, w2i, w2o, g3, b3, wg, wp, g4, b4)
    return pl.pallas_call(
        _post_kernel,
        grid=(n // TOKEN_TILE,),
        in_specs=[_rows(HG_WIDTH), _rows(SSD_WIDTH), _rows(D_MODEL), _rows(PLE_DIM)]
        + [_resident(w.shape) for w in weights],
        out_specs=_rows(D_MODEL),
        out_shape=jax.ShapeDtypeStruct((n, D_MODEL), F32),
        compiler_params=pltpu.CompilerParams(
            dimension_semantics=("arbitrary",), vmem_limit_bytes=V7X_VMEM_LIMIT_BYTES),
        name="post_outproj_ffn_ple",
    )(oh, osd, x1, p2d, *weights)


def _row(v):
    return v.reshape(1, -1).astype(F32)


def kernel(x, p, ffn1_w_in, ffn1_w_out, ln1_g, ln1_b, w_in_mix, hgrn_lb_logits, hgrn_norm_g,
           ssd_conv_w, ssd_conv_b, ssd_dt_bias, ssd_a_log, ssd_d, ssd_norm_g, w_out_mix,
           ln2_g, ln2_b, ffn2_w_in, ffn2_w_out, ln3_g, ln3_b, ple_w_proj, ple_w_gate,
           ln4_g, ln4_b):
    bsz, seqlen, _ = x.shape
    n = bsz * seqlen
    assert DEPTH == 1 and n % TOKEN_TILE == 0
    assert seqlen % SEQ_TILE == 0 and seqlen % HGRN_TILE == 0
    h = x.reshape(n, D_MODEL)
    for i in range(DEPTH):
        pad = MIX_COLS_PADDED - w_in_mix.shape[-1]
        wmix = jnp.pad(w_in_mix[i], ((0, 0), (0, pad))).astype(BF16)
        x1, x1b = _pre_call(h, ffn1_w_in[i].astype(BF16), ffn1_w_out[i].astype(BF16),
                            _row(ln1_g[i]), _row(ln1_b[i]))
        hq, hf, hi, hg, sz, sxbc, sdt = _proj_call(x1b, wmix)

        def seq(t):
            return t.reshape(bsz, seqlen, t.shape[-1])

        o_h = _hgrn_call(hgrn_lb_logits.astype(F32), _row(hgrn_norm_g[i]),
                         seq(hq), seq(hf), seq(hi), seq(hg))
        head_pad = DT_PAD - SSD_HEADS
        o_s = _ssd_call(
            ssd_conv_w[i].astype(F32), _row(ssd_conv_b[i]),
            jnp.pad(_row(ssd_dt_bias[i]), ((0, 0), (0, head_pad))),
            jnp.pad(_row(ssd_a_log[i]), ((0, 0), (0, head_pad))),
            _row(jnp.repeat(ssd_d[i], SSD_HEADDIM)), _row(ssd_norm_g[i]),
            seq(sz), seq(sxbc), seq(sdt))
        h = _post_call(
            o_h.reshape(n, HG_WIDTH), o_s.reshape(n, SSD_WIDTH), x1, p[i].reshape(n, PLE_DIM),
            w_out_mix[i].astype(BF16), _row(ln2_g[i]), _row(ln2_b[i]),
            ffn2_w_in[i].astype(BF16), ffn2_w_out[i].astype(BF16), _row(ln3_g[i]), _row(ln3_b[i]),
            ple_w_gate[i].astype(BF16), ple_w_proj[i].astype(BF16), _row(ln4_g[i]), _row(ln4_b[i]))
    return h.reshape(bsz, seqlen, D_MODEL)
```

```python
import jax
import jax.numpy as jnp
from jax import lax
from jax.experimental import pallas as pl
from jax.experimental.pallas import tpu as pltpu

F32 = jnp.float32
BF16 = jnp.bfloat16

DEPTH = 1
D_MODEL = 1024
D_FF = 2816
CHUNK = 64
PLE_DIM = 256
HG_HEADS = 4
HG_DK = 128
HG_DV = 128
HG_WIDTH = HG_HEADS * HG_DV
HG_FDIM = HG_HEADS * HG_DK
SSD_WIDTH = 512
SSD_HEADDIM = 64
SSD_HEADS = SSD_WIDTH // SSD_HEADDIM
SSD_GROUPS = 2
SSD_HPG = SSD_HEADS // SSD_GROUPS
SSD_STATE = 128
SSD_CONV = 4
SSD_CONV_CH = SSD_WIDTH + 2 * SSD_GROUPS * SSD_STATE
SSD_GROUP_W = SSD_HPG * SSD_HEADDIM
DEEPNORM_ALPHA = (2.0 * DEPTH) ** 0.25
LN_EPS = 1e-5
RMS_EPS = 1e-6
LOG2_E = 1.4426950408889634

LANES = 128
SUBLANES = 8
V7X_MXU_DIM = 256
V7X_VMEM_LIMIT_BYTES = 56 * 1024 * 1024

DT_PAD = LANES
MIX_COLS_PADDED = 2 * HG_FDIM + 2 * HG_WIDTH + SSD_WIDTH + SSD_CONV_CH + DT_PAD
TOKEN_TILE = 512
SUBTILE = 256
_SUBTILES = tuple(slice(r, r + SUBTILE) for r in range(0, TOKEN_TILE, SUBTILE))
FF_SPLITS = (0, (D_FF // V7X_MXU_DIM + 1) // 2 * V7X_MXU_DIM, D_FF)
SEQ_TILE = 256
HGRN_TILE = 512
SUB = 16
NSUB = CHUNK // SUB
HALF = SUB // 2
NEG_BIG = -1e30
SSD_Q = 128
CONV_ROWS = 64


def _sigmoid(x):
    return 1.0 / (1.0 + jnp.exp(-x))


def _silu(x):
    return x * _sigmoid(x)


def _softplus(x):
    return jnp.maximum(x, 0.0) + jnp.log1p(jnp.exp(-jnp.abs(x)))


def _dot(a, b):
    return jnp.dot(a, b, preferred_element_type=F32)


def _dot_nt(a, b):
    return lax.dot_general(a, b, (((1,), (1,)), ((), ())), preferred_element_type=F32)


def _dot_tn(a, b):
    return lax.dot_general(a, b, (((0,), (0,)), ((), ())), preferred_element_type=F32)


def _split3(x):
    hi = x.astype(BF16)
    r1 = x - hi.astype(F32)
    mid = r1.astype(BF16)
    lo = (r1 - mid.astype(F32)).astype(BF16)
    return hi, mid, lo


def _dot_sel(sel, x):
    hi, mid, lo = _split3(x)
    return _dot(sel, hi) + _dot(sel, mid) + _dot(sel, lo)


def _dot_x_sel(x, sel):
    hi, mid, lo = _split3(x)
    return _dot(hi, sel) + _dot(mid, sel) + _dot(lo, sel)


def _layer_norm(y, g, b):
    mu = jnp.mean(y, axis=-1, keepdims=True)
    d = y - mu
    var = jnp.mean(d * d, axis=-1, keepdims=True)
    return d * lax.rsqrt(var + LN_EPS) * g + b


def _ffn_ln(x, w_in_ref, w_out_ref, g, b):
    xb = x.astype(BF16)
    acc = None
    for c0, c1 in zip(FF_SPLITS[:-1], FF_SPLITS[1:]):
        gate = _dot(xb, w_in_ref[:, c0:c1])
        up = _dot(xb, w_in_ref[:, D_FF + c0:D_FF + c1])
        act = (_silu(gate) * up).astype(BF16)
        part = _dot(act, w_out_ref[c0:c1, :])
        acc = part if acc is None else acc + part
    return _layer_norm(DEEPNORM_ALPHA * x + 0.5 * acc, g, b)


def _pre_kernel(x_ref, w1i_ref, w1o_ref, g1_ref, b1_ref, x1_ref, x1b_ref):
    for rows in _SUBTILES:
        x1 = _ffn_ln(x_ref[rows, :], w1i_ref, w1o_ref, g1_ref[...], b1_ref[...])
        x1_ref[rows, :] = x1
        x1b_ref[rows, :] = x1.astype(BF16)


def _resident(shape):
    return pl.BlockSpec(shape, lambda *_: (0,) * len(shape), pipeline_mode=pl.Buffered(1))


def _rows(width):
    return pl.BlockSpec((TOKEN_TILE, width), lambda i: (i, 0))


def _pre_call(x2d, w1i, w1o, g1, b1):
    n = x2d.shape[0]
    return pl.pallas_call(
        _pre_kernel,
        grid=(n // TOKEN_TILE,),
        in_specs=[_rows(D_MODEL), _resident(w1i.shape), _resident(w1o.shape),
                  _resident(g1.shape), _resident(b1.shape)],
        out_specs=[_rows(D_MODEL), _rows(D_MODEL)],
        out_shape=[jax.ShapeDtypeStruct((n, D_MODEL), F32),
                   jax.ShapeDtypeStruct((n, D_MODEL), BF16)],
        compiler_params=pltpu.CompilerParams(
            dimension_semantics=("arbitrary",), vmem_limit_bytes=V7X_VMEM_LIMIT_BYTES),
        name="pre_ffn",
    )(x2d, w1i, w1o, g1, b1)


_MIX_WIDTHS = (HG_FDIM, HG_FDIM, HG_WIDTH, HG_WIDTH, SSD_WIDTH, SSD_CONV_CH, DT_PAD)
_MIX_DTYPES = (F32, F32, BF16, F32, F32, F32, F32)


def _proj_kernel(x1b_ref, wmix_ref, *out_refs):
    c = 0
    for ref in out_refs:
        w = ref.shape[-1]
        ref[...] = _dot(x1b_ref[...], wmix_ref[:, c:c + w]).astype(ref.dtype)
        c += w


def _proj_call(x1b, wmix):
    n = x1b.shape[0]
    return pl.pallas_call(
        _proj_kernel,
        grid=(n // TOKEN_TILE,),
        in_specs=[_rows(D_MODEL), _resident(wmix.shape)],
        out_specs=[_rows(w) for w in _MIX_WIDTHS],
        out_shape=[jax.ShapeDtypeStruct((n, w), d) for w, d in zip(_MIX_WIDTHS, _MIX_DTYPES)],
        compiler_params=pltpu.CompilerParams(
            dimension_semantics=("arbitrary",), vmem_limit_bytes=V7X_VMEM_LIMIT_BYTES),
        name="mix_inproj",
    )(x1b, wmix)


def _hgrn_kernel(lbl_ref, ng_ref, suffix_ref, sumsel_ref, q_ref, f_ref, v_ref, g_ref, o_ref,
                 st_ref, f_sc, kk_sc, p_sc, a_sc, qa_sc, ke_sc, fall_sc):
    @pl.when(pl.program_id(1) == 0)
    def _():
        st_ref[...] = jnp.zeros_like(st_ref)

    logits = lbl_ref[...]
    ex = jnp.exp(logits - jnp.max(logits, axis=0, keepdims=True))
    lb = ex[0:1, :] / jnp.sum(ex, axis=0, keepdims=True)
    ng = ng_ref[...]

    row = lax.broadcasted_iota(jnp.int32, (CHUNK, CHUNK), 0)
    col = lax.broadcasted_iota(jnp.int32, (CHUNK, CHUNK), 1)
    diag_mask = (col // SUB == row // SUB) & (col <= row)
    tloc_lo = lax.broadcasted_iota(jnp.int32, (HALF, HG_DK), 0)
    tloc_hi = tloc_lo + HALF
    zero_half = jnp.zeros((HALF, HG_DK), F32)
    zero_blk = jnp.zeros((SUB, HG_DK), BF16)

    def blk(x, i):
        return x[i * SUB:(i + 1) * SUB, :]

    def scores(c):
        rows = pl.ds(pl.multiple_of(c * CHUNK, CHUNK), CHUNK)
        f = lb + (1.0 - lb) * _sigmoid(f_ref[0, rows, :])
        kk = 1.0 - f
        qq = _silu(q_ref[0, rows, :])
        f_sc[...] = f
        kk_sc[...] = kk
        d = jnp.exp2(_dot_sel(suffix_ref[...], jnp.log2(f)))
        kd = kk * d
        fblk = [f[i * SUB:i * SUB + 1, :] * d[i * SUB:i * SUB + 1, :] for i in range(NSUB)]

        qp = []
        for i in range(NSUB):
            hsl = [slice(h * HG_DK, (h + 1) * HG_DK) for h in range(HG_HEADS)]
            q_lo = [qq[i * SUB:i * SUB + HALF, hs] for hs in hsl]
            q_hi = [qq[i * SUB + HALF:(i + 1) * SUB, hs] for hs in hsl]
            w_lo, w_hi = list(q_lo), list(q_hi)
            for s in range(SUB - 1, -1, -1):
                r = i * SUB + s
                keep_hi = tloc_hi >= s
                keep_lo = tloc_lo >= s
                for h, hs in enumerate(hsl):
                    k_s = jnp.broadcast_to(kk_sc[r:r + 1, hs], (HALF, HG_DK))
                    f_s = jnp.broadcast_to(f_sc[r:r + 1, hs], (HALF, HG_DK))
                    lo = w_lo[h] * k_s if s < HALF else zero_half
                    piece = jnp.concatenate([lo, w_hi[h] * k_s], axis=0).astype(BF16)
                    p_sc[h * CHUNK + i * SUB:h * CHUNK + (i + 1) * SUB,
                         s * HG_DK:(s + 1) * HG_DK] = piece
                    w_hi[h] = w_hi[h] * f_s
                    if s > HALF:
                        w_hi[h] = jnp.where(keep_hi, w_hi[h], q_hi[h])
                    if s < HALF:
                        w_lo[h] = w_lo[h] * f_s
                        if s > 0:
                            w_lo[h] = jnp.where(keep_lo, w_lo[h], q_lo[h])
            qp.append(jnp.concatenate(
                [jnp.concatenate([w_lo[h], w_hi[h]], axis=0) for h in range(HG_HEADS)], axis=1))
        a_diag = _dot(p_sc[...], sumsel_ref[...])

        f01 = fblk[0] * fblk[1]
        f12 = fblk[1] * fblk[2]
        f23 = fblk[2] * fblk[3]
        f012 = f01 * fblk[2]
        f123 = fblk[1] * f23
        fall_sc[0:1, :] = f01 * f23
        qa_sc[...] = jnp.concatenate(
            [qp[0], qp[1] * fblk[0], qp[2] * f01, qp[3] * f012], axis=0).astype(BF16)
        ke_sc[...] = jnp.concatenate(
            [blk(kd, 0) * f123, blk(kd, 1) * f23, blk(kd, 2) * fblk[3], blk(kd, 3)],
            axis=0).astype(BF16)
        qpb = [x.astype(BF16) for x in qp]
        kdb = [blk(kd, j).astype(BF16) for j in range(NSUB)]
        kd0_f1 = (blk(kd, 0) * fblk[1]).astype(BF16)
        kd0_f12 = (blk(kd, 0) * f12).astype(BF16)
        kd1_f2 = (blk(kd, 1) * fblk[2]).astype(BF16)
        for h in range(HG_HEADS):
            hs = slice(h * HG_DK, (h + 1) * HG_DK)
            z = zero_blk
            lhs = jnp.concatenate([
                jnp.concatenate([z, qpb[1][:, hs], z, z], axis=0),
                jnp.concatenate([z, z, qpb[2][:, hs], z], axis=0),
                jnp.concatenate([z, z, z, qpb[3][:, hs]], axis=0)], axis=1)
            rhs = jnp.concatenate([
                jnp.concatenate([kdb[0][:, hs], z, z, z], axis=0),
                jnp.concatenate([kd0_f1[:, hs], kdb[1][:, hs], z, z], axis=0),
                jnp.concatenate([kd0_f12[:, hs], kd1_f2[:, hs], kdb[2][:, hs], z], axis=0)],
                axis=1)
            a = _dot_nt(lhs, rhs) + jnp.where(
                diag_mask, a_diag[h * CHUNK:(h + 1) * CHUNK, :], 0.0)
            a_sc[h] = a.astype(BF16)

    def apply(c):
        rows = pl.ds(pl.multiple_of(c * CHUNK, CHUNK), CHUNK)
        f_all = fall_sc[0:1, :]
        for h in range(HG_HEADS):
            hs = slice(h * HG_DK, (h + 1) * HG_DK)
            vb = v_ref[0, rows, hs]
            st = st_ref[h]
            o = _dot(a_sc[h], vb) + _dot_nt(qa_sc[:, hs], st.astype(BF16))
            st_ref[h] = st * f_all[:, hs] + _dot_tn(vb, ke_sc[:, hs])
            o = o * lax.rsqrt(jnp.mean(o * o, axis=-1, keepdims=True) + RMS_EPS) * ng
            o = o * _silu(g_ref[0, rows, hs])
            o_ref[0, rows, hs] = o.astype(o_ref.dtype)

    n_chunks = HGRN_TILE // CHUNK
    scores(0)

    def body(c, carry):
        apply(c)
        scores(c + 1)
        return carry

    lax.fori_loop(0, n_chunks - 1, body, 0)
    apply(n_chunks - 1)


def _seq_spec(tile, width):
    return pl.BlockSpec((1, tile, width), lambda b, t: (b, t, 0))


def _hgrn_selectors():
    row = lax.broadcasted_iota(jnp.int32, (CHUNK, CHUNK), 0)
    col = lax.broadcasted_iota(jnp.int32, (CHUNK, CHUNK), 1)
    suffix = ((col // SUB == row // SUB) & (col > row)).astype(BF16)
    srow = lax.broadcasted_iota(jnp.int32, (SUB * HG_DK, CHUNK), 0)
    scol = lax.broadcasted_iota(jnp.int32, (SUB * HG_DK, CHUNK), 1)
    sumsel = (srow // HG_DK == scol % SUB).astype(BF16)
    return suffix, sumsel


def _hgrn_call(lb_logits, norm_g, hq, hf, hi, hg):
    bsz, seqlen, _ = hq.shape
    suffix, sumsel = _hgrn_selectors()
    return pl.pallas_call(
        _hgrn_kernel,
        grid=(bsz, seqlen // HGRN_TILE),
        in_specs=[_resident(lb_logits.shape), _resident(norm_g.shape),
                  _resident(suffix.shape), _resident(sumsel.shape),
                  _seq_spec(HGRN_TILE, HG_FDIM), _seq_spec(HGRN_TILE, HG_FDIM),
                  _seq_spec(HGRN_TILE, HG_WIDTH), _seq_spec(HGRN_TILE, HG_WIDTH)],
        out_specs=_seq_spec(HGRN_TILE, HG_WIDTH),
        out_shape=jax.ShapeDtypeStruct((bsz, seqlen, HG_WIDTH), BF16),
        scratch_shapes=[
            pltpu.VMEM((HG_HEADS, HG_DV, HG_DK), F32),
            pltpu.VMEM((CHUNK, HG_FDIM), F32),
            pltpu.VMEM((CHUNK, HG_FDIM), F32),
            pltpu.VMEM((HG_HEADS * CHUNK, SUB * HG_DK), BF16),
            pltpu.VMEM((HG_HEADS, CHUNK, CHUNK), BF16),
            pltpu.VMEM((CHUNK, HG_FDIM), BF16),
            pltpu.VMEM((CHUNK, HG_FDIM), BF16),
            pltpu.VMEM((SUBLANES, HG_FDIM), F32),
        ],
        compiler_params=pltpu.CompilerParams(
            dimension_semantics=("arbitrary", "arbitrary"), vmem_limit_bytes=V7X_VMEM_LIMIT_BYTES),
        name="hgrn2_mixer",
    )(lb_logits, norm_g, suffix, sumsel, hq, hf, hi, hg)


def _ssd_kernel(cw_ref, cb_ref, dtb_ref, alog_ref, dexp_ref, ng_ref,
                expand_ref, tri_ref, upper_ref,
                z_ref, xbc_ref, dt_ref, o_ref, ht_ref, xbuf, xc_sc):
    @pl.when(pl.program_id(1) == 0)
    def _():
        ht_ref[...] = jnp.zeros_like(ht_ref)
        xbuf[0:SUBLANES, :] = jnp.zeros((SUBLANES, SSD_CONV_CH), F32)

    dexp = dexp_ref[...]
    ng = ng_ref[...]
    rate = -jnp.exp(alog_ref[...]) * LOG2_E

    xbuf[SUBLANES:SUBLANES + SEQ_TILE, :] = xbc_ref[0]
    for r0 in range(0, SEQ_TILE, CONV_ROWS):
        win = xbuf[r0:r0 + CONV_ROWS + SUBLANES, :]
        conv = cb_ref[...] + cw_ref[SSD_CONV - 1:SSD_CONV, :] * win[SUBLANES:, :]
        for d in range(1, SSD_CONV):
            shifted = pltpu.roll(win, d, axis=0)[SUBLANES:, :]
            conv = conv + cw_ref[SSD_CONV - 1 - d:SSD_CONV - d, :] * shifted
        xc_sc[r0:r0 + CONV_ROWS, :] = _silu(conv)
    xbuf[0:SUBLANES, :] = xbuf[SEQ_TILE:SEQ_TILE + SUBLANES, :]

    dt = _softplus(dt_ref[0] + dtb_ref[...])
    a = dt * rate
    expand = expand_ref[...]
    acum_h = _dot_sel(tri_ref[...], a)
    acum_all = _dot_x_sel(acum_h, expand)
    acum_t_all = _dot_x_sel(a.T, upper_ref[...])
    dtx_all = xc_sc[:, 0:SSD_WIDTH] * _dot_x_sel(dt, expand)

    row = lax.broadcasted_iota(jnp.int32, (SSD_Q, LANES), 0)
    lane = lax.broadcasted_iota(jnp.int32, (SSD_Q, LANES), 1)
    key = lane % SSD_HEADDIM
    first_head = lane < SSD_HEADDIM
    krow = lax.broadcasted_iota(jnp.int32, (SSD_HEADDIM, LANES), 1) < SSD_HEADDIM

    for c in range(SEQ_TILE // SSD_Q):
        rows = slice(c * SSD_Q, (c + 1) * SSD_Q)
        acum = acum_all[rows, :]
        acum_t = acum_t_all[:, rows]
        xs = xc_sc[rows, 0:SSD_WIDTH]
        dtx = dtx_all[rows, :]
        a_last = acum[SSD_Q - 1:SSD_Q, :]
        dtx_end = (dtx * jnp.exp2(a_last - acum)).astype(BF16)
        dtxb = dtx.astype(BF16)
        e_cum = jnp.exp2(acum)
        e_last = jnp.exp2(a_last)

        ys = []
        for g in range(SSD_GROUPS):
            gs = slice(g * SSD_GROUP_W, (g + 1) * SSD_GROUP_W)
            b0 = SSD_WIDTH + g * SSD_STATE
            c0 = SSD_WIDTH + SSD_GROUPS * SSD_STATE + g * SSD_STATE
            bm = xc_sc[rows, b0:b0 + SSD_STATE].astype(BF16)
            cm = xc_sc[rows, c0:c0 + SSD_STATE].astype(BF16)
            ht = ht_ref[g]
            y_grp = _dot(cm, ht.astype(BF16)) * e_cum[:, gs]
            y_pairs = [None] * (SSD_HPG // 2)
            for half in range(SSD_Q // SSD_HEADDIM):
                s0 = half * SSD_HEADDIM
                bm_half = bm[s0:s0 + SSD_HEADDIM, :]
                cb2 = _dot_nt(cm, jnp.concatenate([bm_half, bm_half], axis=0))
                causal2 = row >= key + s0
                for pr in range(SSD_HPG // 2):
                    h = g * SSD_HPG + 2 * pr
                    lanes = slice(h * SSD_HEADDIM, (h + 2) * SSD_HEADDIM)
                    keys_a = acum_t[h:h + 1, s0:s0 + SSD_HEADDIM]
                    keys_b = acum_t[h + 1:h + 2, s0:s0 + SSD_HEADDIM]
                    keys_t = jnp.where(first_head[0:1, :],
                                       jnp.concatenate([keys_a, keys_a], axis=1),
                                       jnp.concatenate([keys_b, keys_b], axis=1))
                    seg = acum[:, lanes] - keys_t
                    lmat = jnp.exp2(jnp.where(causal2, seg, NEG_BIG))
                    x_half = dtxb[s0:s0 + SSD_HEADDIM, lanes]
                    zero = jnp.zeros_like(x_half)
                    rhs = jnp.concatenate([jnp.where(krow, x_half, zero),
                                           jnp.where(krow, zero, x_half)], axis=0)
                    part = _dot((cb2 * lmat).astype(BF16), rhs)
                    y_pairs[pr] = part if y_pairs[pr] is None else y_pairs[pr] + part
            ys.append(jnp.concatenate(y_pairs, axis=1) + y_grp)
            ht_ref[g] = ht * e_last[:, gs] + _dot_tn(bm, dtx_end[:, gs])
        y = jnp.concatenate(ys, axis=1) + dexp * xs
        y = y * _silu(z_ref[0, rows, :])
        y = y * lax.rsqrt(jnp.mean(y * y, axis=-1, keepdims=True) + RMS_EPS) * ng
        o_ref[0, rows, :] = y.astype(o_ref.dtype)


def _ssd_selectors():
    hrow = lax.broadcasted_iota(jnp.int32, (DT_PAD, SSD_WIDTH), 0)
    hcol = lax.broadcasted_iota(jnp.int32, (DT_PAD, SSD_WIDTH), 1)
    expand = (hrow == hcol // SSD_HEADDIM).astype(BF16)
    row = lax.broadcasted_iota(jnp.int32, (SEQ_TILE, SEQ_TILE), 0)
    col = lax.broadcasted_iota(jnp.int32, (SEQ_TILE, SEQ_TILE), 1)
    same_chunk = row // SSD_Q == col // SSD_Q
    tri = (same_chunk & (col <= row)).astype(BF16)
    upper = (same_chunk & (row <= col)).astype(BF16)
    return expand, tri, upper


def _ssd_call(conv_w, conv_b, dt_bias, a_log, d_exp, norm_g, sz, sxbc, sdt):
    bsz, seqlen, _ = sz.shape
    params = (conv_w, conv_b, dt_bias, a_log, d_exp, norm_g) + _ssd_selectors()
    return pl.pallas_call(
        _ssd_kernel,
        grid=(bsz, seqlen // SEQ_TILE),
        in_specs=[_resident(p.shape) for p in params]
        + [_seq_spec(SEQ_TILE, SSD_WIDTH), _seq_spec(SEQ_TILE, SSD_CONV_CH),
           _seq_spec(SEQ_TILE, DT_PAD)],
        out_specs=_seq_spec(SEQ_TILE, SSD_WIDTH),
        out_shape=jax.ShapeDtypeStruct((bsz, seqlen, SSD_WIDTH), BF16),
        scratch_shapes=[
            pltpu.VMEM((SSD_GROUPS, SSD_STATE, SSD_GROUP_W), F32),
            pltpu.VMEM((SEQ_TILE + SUBLANES, SSD_CONV_CH), F32),
            pltpu.VMEM((SEQ_TILE, SSD_CONV_CH), F32),
        ],
        compiler_params=pltpu.CompilerParams(
            dimension_semantics=("arbitrary", "arbitrary"), vmem_limit_bytes=V7X_VMEM_LIMIT_BYTES),
        name="ssd_mixer",
    )(*params, sz, sxbc, sdt)


def _post_kernel(oh_ref, os_ref, x1_ref, p_ref, wo_ref, g2_ref, b2_ref,
                 w2i_ref, w2o_ref, g3_ref, b3_ref, wg_ref, wp_ref, g4_ref, b4_ref, out_ref):
    x2 = []
    for rows in _SUBTILES:
        mix = (_dot(oh_ref[rows, :], wo_ref[0:HG_WIDTH, :])
               + _dot(os_ref[rows, :], wo_ref[HG_WIDTH:, :]))
        x2.append(_layer_norm(DEEPNORM_ALPHA * x1_ref[rows, :] + mix, g2_ref[...], b2_ref[...]))
    x3 = [_ffn_ln(x, w2i_ref, w2o_ref, g3_ref[...], b3_ref[...]) for x in x2]
    for rows, x in zip(_SUBTILES, x3):
        gate = _sigmoid(_dot(x.astype(BF16), wg_ref[...]))
        ple = gate * _dot(p_ref[rows, :].astype(BF16), wp_ref[...])
        out_ref[rows, :] = _layer_norm(DEEPNORM_ALPHA * x + ple, g4_ref[...], b4_ref[...])


def _post_call(oh, osd, x1, p2d, wo, g2, b2, w2i, w2o, g3, b3, wg, wp, g4, b4):
    n = x1.shape[0]
    weights = (wo, g2, b2, w2i, w2o, g3, b3, wg, wp, g4, b4)
    return pl.pallas_call(
        _post_kernel,
        grid=(n // TOKEN_TILE,),
        in_specs=[_rows(HG_WIDTH), _rows(SSD_WIDTH), _rows(D_MODEL), _rows(PLE_DIM)]
        + [_resident(w.shape) for w in weights],
        out_specs=_rows(D_MODEL),
        out_shape=jax.ShapeDtypeStruct((n, D_MODEL), F32),
        compiler_params=pltpu.CompilerParams(
            dimension_semantics=("arbitrary",), vmem_limit_bytes=V7X_VMEM_LIMIT_BYTES),
        name="post_outproj_ffn_ple",
    )(oh, osd, x1, p2d, *weights)


def _row(v):
    return v.reshape(1, -1).astype(F32)


def kernel(x, p, ffn1_w_in, ffn1_w_out, ln1_g, ln1_b, w_in_mix, hgrn_lb_logits, hgrn_norm_g,
           ssd_conv_w, ssd_conv_b, ssd_dt_bias, ssd_a_log, ssd_d, ssd_norm_g, w_out_mix,
           ln2_g, ln2_b, ffn2_w_in, ffn2_w_out, ln3_g, ln3_b, ple_w_proj, ple_w_gate,
           ln4_g, ln4_b):
    bsz, seqlen, _ = x.shape
    n = bsz * seqlen
    assert DEPTH == 1 and n % TOKEN_TILE == 0
    assert seqlen % SEQ_TILE == 0 and seqlen % HGRN_TILE == 0
    h = x.reshape(n, D_MODEL)
    for i in range(DEPTH):
        pad = MIX_COLS_PADDED - w_in_mix.shape[-1]
        wmix = jnp.pad(w_in_mix[i], ((0, 0), (0, pad))).astype(BF16)
        x1, x1b = _pre_call(h, ffn1_w_in[i].astype(BF16), ffn1_w_out[i].astype(BF16),
                            _row(ln1_g[i]), _row(ln1_b[i]))
        hq, hf, hi, hg, sz, sxbc, sdt = _proj_call(x1b, wmix)

        def seq(t):
            return t.reshape(bsz, seqlen, t.shape[-1])

        o_h = _hgrn_call(hgrn_lb_logits.astype(F32), _row(hgrn_norm_g[i]),
                         seq(hq), seq(hf), seq(hi), seq(hg))
        head_pad = DT_PAD - SSD_HEADS
        o_s = _ssd_call(
            ssd_conv_w[i].astype(F32), _row(ssd_conv_b[i]),
            jnp.pad(_row(ssd_dt_bias[i]), ((0, 0), (0, head_pad))),
            jnp.pad(_row(ssd_a_log[i]), ((0, 0), (0, head_pad))),
            _row(jnp.repeat(ssd_d[i], SSD_HEADDIM)), _row(ssd_norm_g[i]),
            seq(sz), seq(sxbc), seq(sdt))
        h = _post_call(
            o_h.reshape(n, HG_WIDTH), o_s.reshape(n, SSD_WIDTH), x1, p[i].reshape(n, PLE_DIM),
            w_out_mix[i].astype(BF16), _row(ln2_g[i]), _row(ln2_b[i]),
            ffn2_w_in[i].astype(BF16), ffn2_w_out[i].astype(BF16), _row(ln3_g[i]), _row(ln3_b[i]),
            ple_w_gate[i].astype(BF16), ple_w_proj[i].astype(BF16), _row(ln4_g[i]), _row(ln4_b[i]))
    return h.reshape(bsz, seqlen, D_MODEL)
```

```python
import jax
import jax.numpy as jnp
from jax import lax
from jax.experimental import pallas as pl
from jax.experimental.pallas import tpu as pltpu

F32 = jnp.float32
BF16 = jnp.bfloat16

DEPTH = 1
D_MODEL = 1024
D_FF = 2816
CHUNK = 64
PLE_DIM = 256
HG_HEADS = 4
HG_DK = 128
HG_DV = 128
HG_WIDTH = HG_HEADS * HG_DV
HG_FDIM = HG_HEADS * HG_DK
SSD_WIDTH = 512
SSD_HEADDIM = 64
SSD_HEADS = SSD_WIDTH // SSD_HEADDIM
SSD_GROUPS = 2
SSD_HPG = SSD_HEADS // SSD_GROUPS
SSD_STATE = 128
SSD_CONV = 4
SSD_CONV_CH = SSD_WIDTH + 2 * SSD_GROUPS * SSD_STATE
SSD_GROUP_W = SSD_HPG * SSD_HEADDIM
DEEPNORM_ALPHA = (2.0 * DEPTH) ** 0.25
LN_EPS = 1e-5
RMS_EPS = 1e-6
LOG2_E = 1.4426950408889634

LANES = 128
SUBLANES = 8
V7X_MXU_DIM = 256
V7X_VMEM_LIMIT_BYTES = 56 * 1024 * 1024

DT_PAD = LANES
MIX_COLS_PADDED = 2 * HG_FDIM + 2 * HG_WIDTH + SSD_WIDTH + SSD_CONV_CH + DT_PAD
TOKEN_TILE = 512
SUBTILE = 256
_SUBTILES = tuple(slice(r, r + SUBTILE) for r in range(0, TOKEN_TILE, SUBTILE))
FF_SPLITS = (0, (D_FF // V7X_MXU_DIM + 1) // 2 * V7X_MXU_DIM, D_FF)
SEQ_TILE = 256
HGRN_TILE = 2048
SUB = 16
NSUB = CHUNK // SUB
HALF = SUB // 2
NEG_BIG = -1e30
SSD_Q = 128
CONV_ROWS = 64


def _sigmoid(x):
    return 1.0 / (1.0 + jnp.exp(-x))


def _silu(x):
    return x * _sigmoid(x)


def _softplus(x):
    return jnp.maximum(x, 0.0) + jnp.log1p(jnp.exp(-jnp.abs(x)))


def _dot(a, b):
    return jnp.dot(a, b, preferred_element_type=F32)


def _dot_nt(a, b):
    return lax.dot_general(a, b, (((1,), (1,)), ((), ())), preferred_element_type=F32)


def _dot_tn(a, b):
    return lax.dot_general(a, b, (((0,), (0,)), ((), ())), preferred_element_type=F32)


def _split3(x):
    hi = x.astype(BF16)
    r1 = x - hi.astype(F32)
    mid = r1.astype(BF16)
    lo = (r1 - mid.astype(F32)).astype(BF16)
    return hi, mid, lo


def _dot_sel(sel, x):
    hi, mid, lo = _split3(x)
    return _dot(sel, hi) + _dot(sel, mid) + _dot(sel, lo)


def _dot_x_sel(x, sel):
    hi, mid, lo = _split3(x)
    return _dot(hi, sel) + _dot(mid, sel) + _dot(lo, sel)


def _layer_norm(y, g, b):
    mu = jnp.mean(y, axis=-1, keepdims=True)
    d = y - mu
    var = jnp.mean(d * d, axis=-1, keepdims=True)
    return d * lax.rsqrt(var + LN_EPS) * g + b


def _ffn_ln(x, w_in_ref, w_out_ref, g, b):
    xb = x.astype(BF16)
    acc = None
    for c0, c1 in zip(FF_SPLITS[:-1], FF_SPLITS[1:]):
        gate = _dot(xb, w_in_ref[:, c0:c1])
        up = _dot(xb, w_in_ref[:, D_FF + c0:D_FF + c1])
        act = (_silu(gate) * up).astype(BF16)
        part = _dot(act, w_out_ref[c0:c1, :])
        acc = part if acc is None else acc + part
    return _layer_norm(DEEPNORM_ALPHA * x + 0.5 * acc, g, b)


def _pre_kernel(x_ref, w1i_ref, w1o_ref, g1_ref, b1_ref, x1_ref, x1b_ref):
    for rows in _SUBTILES:
        x1 = _ffn_ln(x_ref[rows, :], w1i_ref, w1o_ref, g1_ref[...], b1_ref[...])
        x1_ref[rows, :] = x1
        x1b_ref[rows, :] = x1.astype(BF16)


def _resident(shape):
    return pl.BlockSpec(shape, lambda *_: (0,) * len(shape), pipeline_mode=pl.Buffered(1))


def _rows(width):
    return pl.BlockSpec((TOKEN_TILE, width), lambda i: (i, 0))


def _pre_call(x2d, w1i, w1o, g1, b1):
    n = x2d.shape[0]
    return pl.pallas_call(
        _pre_kernel,
        grid=(n // TOKEN_TILE,),
        in_specs=[_rows(D_MODEL), _resident(w1i.shape), _resident(w1o.shape),
                  _resident(g1.shape), _resident(b1.shape)],
        out_specs=[_rows(D_MODEL), _rows(D_MODEL)],
        out_shape=[jax.ShapeDtypeStruct((n, D_MODEL), F32),
                   jax.ShapeDtypeStruct((n, D_MODEL), BF16)],
        compiler_params=pltpu.CompilerParams(
            dimension_semantics=("arbitrary",), vmem_limit_bytes=V7X_VMEM_LIMIT_BYTES),
        name="pre_ffn",
    )(x2d, w1i, w1o, g1, b1)


_MIX_WIDTHS = (HG_FDIM, HG_FDIM, HG_WIDTH, HG_WIDTH, SSD_WIDTH, SSD_CONV_CH, DT_PAD)
_MIX_DTYPES = (F32, F32, BF16, F32, F32, F32, F32)


def _proj_kernel(x1b_ref, wmix_ref, *out_refs):
    c = 0
    for ref in out_refs:
        w = ref.shape[-1]
        ref[...] = _dot(x1b_ref[...], wmix_ref[:, c:c + w]).astype(ref.dtype)
        c += w


def _proj_call(x1b, wmix):
    n = x1b.shape[0]
    return pl.pallas_call(
        _proj_kernel,
        grid=(n // TOKEN_TILE,),
        in_specs=[_rows(D_MODEL), _resident(wmix.shape)],
        out_specs=[_rows(w) for w in _MIX_WIDTHS],
        out_shape=[jax.ShapeDtypeStruct((n, w), d) for w, d in zip(_MIX_WIDTHS, _MIX_DTYPES)],
        compiler_params=pltpu.CompilerParams(
            dimension_semantics=("arbitrary",), vmem_limit_bytes=V7X_VMEM_LIMIT_BYTES),
        name="mix_inproj",
    )(x1b, wmix)


def _hgrn_kernel(lbl_ref, ng_ref, suffix_ref, sumsel_ref, q_ref, f_ref, v_ref, g_ref, o_ref,
                 st_ref, f_sc, kk_sc, p_sc, a_sc, qa_sc, ke_sc, fall_sc):
    @pl.when(pl.program_id(1) == 0)
    def _():
        st_ref[...] = jnp.zeros_like(st_ref)

    logits = lbl_ref[...]
    ex = jnp.exp(logits - jnp.max(logits, axis=0, keepdims=True))
    lb = ex[0:1, :] / jnp.sum(ex, axis=0, keepdims=True)
    ng = ng_ref[...]

    row = lax.broadcasted_iota(jnp.int32, (CHUNK, CHUNK), 0)
    col = lax.broadcasted_iota(jnp.int32, (CHUNK, CHUNK), 1)
    diag_mask = (col // SUB == row // SUB) & (col <= row)
    tloc_lo = lax.broadcasted_iota(jnp.int32, (HALF, HG_DK), 0)
    tloc_hi = tloc_lo + HALF
    zero_half = jnp.zeros((HALF, HG_DK), F32)
    zero_blk = jnp.zeros((SUB, HG_DK), BF16)

    def blk(x, i):
        return x[i * SUB:(i + 1) * SUB, :]

    def scores(c, buf):
        rows = pl.ds(pl.multiple_of(c * CHUNK, CHUNK), CHUNK)
        f = lb + (1.0 - lb) * _sigmoid(f_ref[0, rows, :])
        kk = 1.0 - f
        qq = _silu(q_ref[0, rows, :])
        f_sc[buf] = f
        kk_sc[buf] = kk
        d = jnp.exp2(_dot_sel(suffix_ref[...], jnp.log2(f)))
        kd = kk * d
        fblk = [f[i * SUB:i * SUB + 1, :] * d[i * SUB:i * SUB + 1, :] for i in range(NSUB)]

        qp = []
        a_diag = []
        for i in range(NSUB):
            hsl = [slice(h * HG_DK, (h + 1) * HG_DK) for h in range(HG_HEADS)]
            q_lo = [qq[i * SUB:i * SUB + HALF, hs] for hs in hsl]
            q_hi = [qq[i * SUB + HALF:(i + 1) * SUB, hs] for hs in hsl]
            w_lo, w_hi = list(q_lo), list(q_hi)
            for s in range(SUB - 1, -1, -1):
                r = i * SUB + s
                keep_hi = tloc_hi >= s
                keep_lo = tloc_lo >= s
                for h, hs in enumerate(hsl):
                    k_s = jnp.broadcast_to(kk_sc[buf, r:r + 1, hs], (HALF, HG_DK))
                    f_s = jnp.broadcast_to(f_sc[buf, r:r + 1, hs], (HALF, HG_DK))
                    lo = w_lo[h] * k_s if s < HALF else zero_half
                    piece = jnp.concatenate([lo, w_hi[h] * k_s], axis=0).astype(BF16)
                    p_sc[buf, i, h * SUB:(h + 1) * SUB, s * HG_DK:(s + 1) * HG_DK] = piece
                    w_hi[h] = w_hi[h] * f_s
                    if s > HALF:
                        w_hi[h] = jnp.where(keep_hi, w_hi[h], q_hi[h])
                    if s < HALF:
                        w_lo[h] = w_lo[h] * f_s
                        if s > 0:
                            w_lo[h] = jnp.where(keep_lo, w_lo[h], q_lo[h])
            qp.append(jnp.concatenate(
                [jnp.concatenate([w_lo[h], w_hi[h]], axis=0) for h in range(HG_HEADS)], axis=1))
            kh = SUB * HG_DK // 2
            a_diag.append(_dot(p_sc[buf, i, :, 0:kh], sumsel_ref[0:kh, :])
                          + _dot(p_sc[buf, i, :, kh:], sumsel_ref[kh:, :]))

        f01 = fblk[0] * fblk[1]
        f12 = fblk[1] * fblk[2]
        f23 = fblk[2] * fblk[3]
        f012 = f01 * fblk[2]
        f123 = fblk[1] * f23
        fall_sc[buf, 0:1, :] = f01 * f23
        qa_sc[buf] = jnp.concatenate(
            [qp[0], qp[1] * fblk[0], qp[2] * f01, qp[3] * f012], axis=0).astype(BF16)
        ke_sc[buf] = jnp.concatenate(
            [blk(kd, 0) * f123, blk(kd, 1) * f23, blk(kd, 2) * fblk[3], blk(kd, 3)],
            axis=0).astype(BF16)
        qpb = [x.astype(BF16) for x in qp]
        kdb = [blk(kd, j).astype(BF16) for j in range(NSUB)]
        kd0_f1 = (blk(kd, 0) * fblk[1]).astype(BF16)
        kd0_f12 = (blk(kd, 0) * f12).astype(BF16)
        kd1_f2 = (blk(kd, 1) * fblk[2]).astype(BF16)
        for h in range(HG_HEADS):
            hs = slice(h * HG_DK, (h + 1) * HG_DK)
            z = zero_blk
            lhs = jnp.concatenate([
                jnp.concatenate([z, qpb[1][:, hs], z, z], axis=0),
                jnp.concatenate([z, z, qpb[2][:, hs], z], axis=0),
                jnp.concatenate([z, z, z, qpb[3][:, hs]], axis=0)], axis=1)
            rhs = jnp.concatenate([
                jnp.concatenate([kdb[0][:, hs], z, z, z], axis=0),
                jnp.concatenate([kd0_f1[:, hs], kdb[1][:, hs], z, z], axis=0),
                jnp.concatenate([kd0_f12[:, hs], kd1_f2[:, hs], kdb[2][:, hs], z], axis=0)],
                axis=1)
            a_diag_h = jnp.concatenate([a_diag[i][h * SUB:(h + 1) * SUB, :] for i in range(NSUB)],
                                       axis=0)
            a = _dot_nt(lhs, rhs) + jnp.where(diag_mask, a_diag_h, 0.0)
            a_sc[buf, h] = a.astype(BF16)

    def apply(c, buf):
        rows = pl.ds(pl.multiple_of(c * CHUNK, CHUNK), CHUNK)
        f_all = fall_sc[buf, 0:1, :]
        for h in range(HG_HEADS):
            hs = slice(h * HG_DK, (h + 1) * HG_DK)
            vb = v_ref[0, rows, hs]
            st = st_ref[h]
            o = _dot(a_sc[buf, h], vb) + _dot_nt(qa_sc[buf, :, hs], st.astype(BF16))
            st_ref[h] = st * f_all[:, hs] + _dot_tn(vb, ke_sc[buf, :, hs])
            o = o * lax.rsqrt(jnp.mean(o * o, axis=-1, keepdims=True) + RMS_EPS) * ng
            o = o * _silu(g_ref[0, rows, hs])
            o_ref[0, rows, hs] = o.astype(o_ref.dtype)

    n_pairs = HGRN_TILE // CHUNK // 2
    scores(0, 0)

    def body(j, carry):
        c = 2 * j
        apply(c, 0)
        scores(c + 1, 1)
        scores(c + 2, 0)
        apply(c + 1, 1)
        return carry

    lax.fori_loop(0, n_pairs - 1, body, 0)
    last = 2 * n_pairs - 2
    apply(last, 0)
    scores(last + 1, 1)
    apply(last + 1, 1)


def _seq_spec(tile, width):
    return pl.BlockSpec((1, tile, width), lambda b, t: (b, t, 0))


def _hgrn_selectors():
    row = lax.broadcasted_iota(jnp.int32, (CHUNK, CHUNK), 0)
    col = lax.broadcasted_iota(jnp.int32, (CHUNK, CHUNK), 1)
    suffix = ((col // SUB == row // SUB) & (col > row)).astype(BF16)
    srow = lax.broadcasted_iota(jnp.int32, (SUB * HG_DK, CHUNK), 0)
    scol = lax.broadcasted_iota(jnp.int32, (SUB * HG_DK, CHUNK), 1)
    sumsel = (srow // HG_DK == scol % SUB).astype(BF16)
    return suffix, sumsel


def _hgrn_call(lb_logits, norm_g, hq, hf, hi, hg):
    bsz, seqlen, _ = hq.shape
    suffix, sumsel = _hgrn_selectors()
    return pl.pallas_call(
        _hgrn_kernel,
        grid=(bsz, seqlen // HGRN_TILE),
        in_specs=[_resident(lb_logits.shape), _resident(norm_g.shape),
                  _resident(suffix.shape), _resident(sumsel.shape),
                  _seq_spec(HGRN_TILE, HG_FDIM), _seq_spec(HGRN_TILE, HG_FDIM),
                  _seq_spec(HGRN_TILE, HG_WIDTH), _seq_spec(HGRN_TILE, HG_WIDTH)],
        out_specs=_seq_spec(HGRN_TILE, HG_WIDTH),
        out_shape=jax.ShapeDtypeStruct((bsz, seqlen, HG_WIDTH), BF16),
        scratch_shapes=[
            pltpu.VMEM((HG_HEADS, HG_DV, HG_DK), F32),
            pltpu.VMEM((2, CHUNK, HG_FDIM), F32),
            pltpu.VMEM((2, CHUNK, HG_FDIM), F32),
            pltpu.VMEM((2, NSUB, HG_HEADS * SUB, SUB * HG_DK), BF16),
            pltpu.VMEM((2, HG_HEADS, CHUNK, CHUNK), BF16),
            pltpu.VMEM((2, CHUNK, HG_FDIM), BF16),
            pltpu.VMEM((2, CHUNK, HG_FDIM), BF16),
            pltpu.VMEM((2, SUBLANES, HG_FDIM), F32),
        ],
        compiler_params=pltpu.CompilerParams(
            dimension_semantics=("arbitrary", "arbitrary"), vmem_limit_bytes=V7X_VMEM_LIMIT_BYTES),
        name="hgrn2_mixer",
    )(lb_logits, norm_g, suffix, sumsel, hq, hf, hi, hg)


def _ssd_kernel(cw_ref, cb_ref, dtb_ref, alog_ref, dexp_ref, ng_ref,
                expand_ref, tri_ref, upper_ref,
                z_ref, xbc_ref, dt_ref, o_ref, ht_ref, xbuf, xc_sc):
    @pl.when(pl.program_id(1) == 0)
    def _():
        ht_ref[...] = jnp.zeros_like(ht_ref)
        xbuf[0:SUBLANES, :] = jnp.zeros((SUBLANES, SSD_CONV_CH), F32)

    dexp = dexp_ref[...]
    ng = ng_ref[...]
    rate = -jnp.exp(alog_ref[...]) * LOG2_E

    xbuf[SUBLANES:SUBLANES + SEQ_TILE, :] = xbc_ref[0]
    for r0 in range(0, SEQ_TILE, CONV_ROWS):
        win = xbuf[r0:r0 + CONV_ROWS + SUBLANES, :]
        conv = cb_ref[...] + cw_ref[SSD_CONV - 1:SSD_CONV, :] * win[SUBLANES:, :]
        for d in range(1, SSD_CONV):
            shifted = pltpu.roll(win, d, axis=0)[SUBLANES:, :]
            conv = conv + cw_ref[SSD_CONV - 1 - d:SSD_CONV - d, :] * shifted
        xc_sc[r0:r0 + CONV_ROWS, :] = _silu(conv)
    xbuf[0:SUBLANES, :] = xbuf[SEQ_TILE:SEQ_TILE + SUBLANES, :]

    dt = _softplus(dt_ref[0] + dtb_ref[...])
    a = dt * rate
    expand = expand_ref[...]
    acum_h = _dot_sel(tri_ref[...], a)
    acum_all = _dot_x_sel(acum_h, expand)
    acum_t_all = _dot_x_sel(a.T, upper_ref[...])
    dtx_all = xc_sc[:, 0:SSD_WIDTH] * _dot_x_sel(dt, expand)

    row = lax.broadcasted_iota(jnp.int32, (SSD_Q, LANES), 0)
    lane = lax.broadcasted_iota(jnp.int32, (SSD_Q, LANES), 1)
    key = lane % SSD_HEADDIM
    first_head = lane < SSD_HEADDIM
    krow = lax.broadcasted_iota(jnp.int32, (SSD_HEADDIM, LANES), 1) < SSD_HEADDIM

    for c in range(SEQ_TILE // SSD_Q):
        rows = slice(c * SSD_Q, (c + 1) * SSD_Q)
        acum = acum_all[rows, :]
        acum_t = acum_t_all[:, rows]
        xs = xc_sc[rows, 0:SSD_WIDTH]
        dtx = dtx_all[rows, :]
        a_last = acum[SSD_Q - 1:SSD_Q, :]
        dtx_end = (dtx * jnp.exp2(a_last - acum)).astype(BF16)
        dtxb = dtx.astype(BF16)
        e_cum = jnp.exp2(acum)
        e_last = jnp.exp2(a_last)

        ys = []
        for g in range(SSD_GROUPS):
            gs = slice(g * SSD_GROUP_W, (g + 1) * SSD_GROUP_W)
            b0 = SSD_WIDTH + g * SSD_STATE
            c0 = SSD_WIDTH + SSD_GROUPS * SSD_STATE + g * SSD_STATE
            bm = xc_sc[rows, b0:b0 + SSD_STATE].astype(BF16)
            cm = xc_sc[rows, c0:c0 + SSD_STATE].astype(BF16)
            ht = ht_ref[g]
            y_grp = _dot(cm, ht.astype(BF16)) * e_cum[:, gs]
            y_pairs = [None] * (SSD_HPG // 2)
            for half in range(SSD_Q // SSD_HEADDIM):
                s0 = half * SSD_HEADDIM
                bm_half = bm[s0:s0 + SSD_HEADDIM, :]
                cb2 = _dot_nt(cm, jnp.concatenate([bm_half, bm_half], axis=0))
                causal2 = row >= key + s0
                for pr in range(SSD_HPG // 2):
                    h = g * SSD_HPG + 2 * pr
                    lanes = slice(h * SSD_HEADDIM, (h + 2) * SSD_HEADDIM)
                    keys_a = acum_t[h:h + 1, s0:s0 + SSD_HEADDIM]
                    keys_b = acum_t[h + 1:h + 2, s0:s0 + SSD_HEADDIM]
                    keys_t = jnp.where(first_head[0:1, :],
                                       jnp.concatenate([keys_a, keys_a], axis=1),
                                       jnp.concatenate([keys_b, keys_b], axis=1))
                    seg = acum[:, lanes] - keys_t
                    lmat = jnp.exp2(jnp.where(causal2, seg, NEG_BIG))
                    x_half = dtxb[s0:s0 + SSD_HEADDIM, lanes]
                    zero = jnp.zeros_like(x_half)
                    rhs = jnp.concatenate([jnp.where(krow, x_half, zero),
                                           jnp.where(krow, zero, x_half)], axis=0)
                    part = _dot((cb2 * lmat).astype(BF16), rhs)
                    y_pairs[pr] = part if y_pairs[pr] is None else y_pairs[pr] + part
            ys.append(jnp.concatenate(y_pairs, axis=1) + y_grp)
            ht_ref[g] = ht * e_last[:, gs] + _dot_tn(bm, dtx_end[:, gs])
        y = jnp.concatenate(ys, axis=1) + dexp * xs
        y = y * _silu(z_ref[0, rows, :])
        y = y * lax.rsqrt(jnp.mean(y * y, axis=-1, keepdims=True) + RMS_EPS) * ng
        o_ref[0, rows, :] = y.astype(o_ref.dtype)


def _ssd_selectors():
    hrow = lax.broadcasted_iota(jnp.int32, (DT_PAD, SSD_WIDTH), 0)
    hcol = lax.broadcasted_iota(jnp.int32, (DT_PAD, SSD_WIDTH), 1)
    expand = (hrow == hcol // SSD_HEADDIM).astype(BF16)
    row = lax.broadcasted_iota(jnp.int32, (SEQ_TILE, SEQ_TILE), 0)
    col = lax.broadcasted_iota(jnp.int32, (SEQ_TILE, SEQ_TILE), 1)
    same_chunk = row // SSD_Q == col // SSD_Q
    tri = (same_chunk & (col <= row)).astype(BF16)
    upper = (same_chunk & (row <= col)).astype(BF16)
    return expand, tri, upper


def _ssd_call(conv_w, conv_b, dt_bias, a_log, d_exp, norm_g, sz, sxbc, sdt):
    bsz, seqlen, _ = sz.shape
    params = (conv_w, conv_b, dt_bias, a_log, d_exp, norm_g) + _ssd_selectors()
    return pl.pallas_call(
        _ssd_kernel,
        grid=(bsz, seqlen // SEQ_TILE),
        in_specs=[_resident(p.shape) for p in params]
        + [_seq_spec(SEQ_TILE, SSD_WIDTH), _seq_spec(SEQ_TILE, SSD_CONV_CH),
           _seq_spec(SEQ_TILE, DT_PAD)],
        out_specs=_seq_spec(SEQ_TILE, SSD_WIDTH),
        out_shape=jax.ShapeDtypeStruct((bsz, seqlen, SSD_WIDTH), BF16),
        scratch_shapes=[
            pltpu.VMEM((SSD_GROUPS, SSD_STATE, SSD_GROUP_W), F32),
            pltpu.VMEM((SEQ_TILE + SUBLANES, SSD_CONV_CH), F32),
            pltpu.VMEM((SEQ_TILE, SSD_CONV_CH), F32),
        ],
        compiler_params=pltpu.CompilerParams(
            dimension_semantics=("arbitrary", "arbitrary"), vmem_limit_bytes=V7X_VMEM_LIMIT_BYTES),
        name="ssd_mixer",
    )(*params, sz, sxbc, sdt)


def _post_kernel(oh_ref, os_ref, x1_ref, p_ref, wo_ref, g2_ref, b2_ref,
                 w2i_ref, w2o_ref, g3_ref, b3_ref, wg_ref, wp_ref, g4_ref, b4_ref, out_ref):
    x2 = []
    for rows in _SUBTILES:
        mix = (_dot(oh_ref[rows, :], wo_ref[0:HG_WIDTH, :])
               + _dot(os_ref[rows, :], wo_ref[HG_WIDTH:, :]))
        x2.append(_layer_norm(DEEPNORM_ALPHA * x1_ref[rows, :] + mix, g2_ref[...], b2_ref[...]))
    x3 = [_ffn_ln(x, w2i_ref, w2o_ref, g3_ref[...], b3_ref[...]) for x in x2]
    for rows, x in zip(_SUBTILES, x3):
        gate = _sigmoid(_dot(x.astype(BF16), wg_ref[...]))
        ple = gate * _dot(p_ref[rows, :].astype(BF16), wp_ref[...])
        out_ref[rows, :] = _layer_norm(DEEPNORM_ALPHA * x + ple, g4_ref[...], b4_ref[...])


def _post_call(oh, osd, x1, p2d, wo, g2, b2, w2i, w2o, g3, b3, wg, wp, g4, b4):
    n = x1.shape[0]
    weights = (wo, g2, b2, w2i, w2o, g3, b3, wg, wp, g4, b4)
    return pl.pallas_call(
        _post_kernel,
        grid=(n // TOKEN_TILE,),
        in_specs=[_rows(HG_WIDTH), _rows(SSD_WIDTH), _rows(D_MODEL), _rows(PLE_DIM)]
        + [_resident(w.shape) for w in weights],
        out_specs=_rows(D_MODEL),
        out_shape=jax.ShapeDtypeStruct((n, D_MODEL), F32),
        compiler_params=pltpu.CompilerParams(
            dimension_semantics=("arbitrary",), vmem_limit_bytes=V7X_VMEM_LIMIT_BYTES),
        name="post_outproj_ffn_ple",
    )(oh, osd, x1, p2d, *weights)


def _row(v):
    return v.reshape(1, -1).astype(F32)


def kernel(x, p, ffn1_w_in, ffn1_w_out, ln1_g, ln1_b, w_in_mix, hgrn_lb_logits, hgrn_norm_g,
           ssd_conv_w, ssd_conv_b, ssd_dt_bias, ssd_a_log, ssd_d, ssd_norm_g, w_out_mix,
           ln2_g, ln2_b, ffn2_w_in, ffn2_w_out, ln3_g, ln3_b, ple_w_proj, ple_w_gate,
           ln4_g, ln4_b):
    bsz, seqlen, _ = x.shape
    n = bsz * seqlen
    assert DEPTH == 1 and n % TOKEN_TILE == 0
    assert seqlen % SEQ_TILE == 0 and seqlen % HGRN_TILE == 0
    h = x.reshape(n, D_MODEL)
    for i in range(DEPTH):
        pad = MIX_COLS_PADDED - w_in_mix.shape[-1]
        wmix = jnp.pad(w_in_mix[i], ((0, 0), (0, pad))).astype(BF16)
        x1, x1b = _pre_call(h, ffn1_w_in[i].astype(BF16), ffn1_w_out[i].astype(BF16),
                            _row(ln1_g[i]), _row(ln1_b[i]))
        hq, hf, hi, hg, sz, sxbc, sdt = _proj_call(x1b, wmix)

        def seq(t):
            return t.reshape(bsz, seqlen, t.shape[-1])

        o_h = _hgrn_call(hgrn_lb_logits.astype(F32), _row(hgrn_norm_g[i]),
                         seq(hq), seq(hf), seq(hi), seq(hg))
        head_pad = DT_PAD - SSD_HEADS
        o_s = _ssd_call(
            ssd_conv_w[i].astype(F32), _row(ssd_conv_b[i]),
            jnp.pad(_row(ssd_dt_bias[i]), ((0, 0), (0, head_pad))),
            jnp.pad(_row(ssd_a_log[i]), ((0, 0), (0, head_pad))),
            _row(jnp.repeat(ssd_d[i], SSD_HEADDIM)), _row(ssd_norm_g[i]),
            seq(sz), seq(sxbc), seq(sdt))
        h = _post_call(
            o_h.reshape(n, HG_WIDTH), o_s.reshape(n, SSD_WIDTH), x1, p[i].reshape(n, PLE_DIM),
            w_out_mix[i].astype(BF16), _row(ln2_g[i]), _row(ln2_b[i]),
            ffn2_w_in[i].astype(BF16), ffn2_w_out[i].astype(BF16), _row(ln3_g[i]), _row(ln3_b[i]),
            ple_w_gate[i].astype(BF16), ple_w_proj[i].astype(BF16), _row(ln4_g[i]), _row(ln4_b[i]))
    return h.reshape(bsz, seqlen, D_MODEL)
```

```python
import jax
import jax.numpy as jnp
from jax import lax
from jax.experimental import pallas as pl
from jax.experimental.pallas import tpu as pltpu

F32 = jnp.float32
BF16 = jnp.bfloat16

DEPTH = 1
D_MODEL = 1024
D_FF = 2816
CHUNK = 64
PLE_DIM = 256
HG_HEADS = 4
HG_DK = 128
HG_DV = 128
HG_WIDTH = HG_HEADS * HG_DV
HG_FDIM = HG_HEADS * HG_DK
SSD_WIDTH = 512
SSD_HEADDIM = 64
SSD_HEADS = SSD_WIDTH // SSD_HEADDIM
SSD_GROUPS = 2
SSD_HPG = SSD_HEADS // SSD_GROUPS
SSD_STATE = 128
SSD_CONV = 4
SSD_CONV_CH = SSD_WIDTH + 2 * SSD_GROUPS * SSD_STATE
SSD_GROUP_W = SSD_HPG * SSD_HEADDIM
DEEPNORM_ALPHA = (2.0 * DEPTH) ** 0.25
LN_EPS = 1e-5
RMS_EPS = 1e-6
LOG2_E = 1.4426950408889634

LANES = 128
SUBLANES = 8
V7X_MXU_DIM = 256
V7X_VMEM_LIMIT_BYTES = 56 * 1024 * 1024

DT_PAD = LANES
MIX_COLS_PADDED = 2 * HG_FDIM + 2 * HG_WIDTH + SSD_WIDTH + SSD_CONV_CH + DT_PAD
TOKEN_TILE = 512
SUBTILE = 256
_SUBTILES = tuple(slice(r, r + SUBTILE) for r in range(0, TOKEN_TILE, SUBTILE))
FF_SPLITS = (0, (D_FF // V7X_MXU_DIM + 1) // 2 * V7X_MXU_DIM, D_FF)
SEQ_TILE = 256
HGRN_TILE = 2048
SUB = 16
NSUB = CHUNK // SUB
HALF = SUB // 2
NEG_BIG = -1e30
SSD_Q = 128
CONV_ROWS = 64


def _sigmoid(x):
    return 1.0 / (1.0 + jnp.exp(-x))


def _silu(x):
    return x * _sigmoid(x)


def _softplus(x):
    return jnp.maximum(x, 0.0) + jnp.log1p(jnp.exp(-jnp.abs(x)))


def _dot(a, b):
    return jnp.dot(a, b, preferred_element_type=F32)


def _dot_nt(a, b):
    return lax.dot_general(a, b, (((1,), (1,)), ((), ())), preferred_element_type=F32)


def _dot_tn(a, b):
    return lax.dot_general(a, b, (((0,), (0,)), ((), ())), preferred_element_type=F32)


def _split3(x):
    hi = x.astype(BF16)
    r1 = x - hi.astype(F32)
    mid = r1.astype(BF16)
    lo = (r1 - mid.astype(F32)).astype(BF16)
    return hi, mid, lo


def _dot_sel(sel, x):
    hi, mid, lo = _split3(x)
    return _dot(sel, hi) + _dot(sel, mid) + _dot(sel, lo)


def _dot_x_sel(x, sel):
    hi, mid, lo = _split3(x)
    return _dot(hi, sel) + _dot(mid, sel) + _dot(lo, sel)


def _layer_norm(y, g, b):
    mu = jnp.mean(y, axis=-1, keepdims=True)
    d = y - mu
    var = jnp.mean(d * d, axis=-1, keepdims=True)
    return d * lax.rsqrt(var + LN_EPS) * g + b


def _ffn_ln(x, w_in_ref, w_out_ref, g, b):
    xb = x.astype(BF16)
    acc = None
    for c0, c1 in zip(FF_SPLITS[:-1], FF_SPLITS[1:]):
        gate = _dot(xb, w_in_ref[:, c0:c1])
        up = _dot(xb, w_in_ref[:, D_FF + c0:D_FF + c1])
        act = (_silu(gate) * up).astype(BF16)
        part = _dot(act, w_out_ref[c0:c1, :])
        acc = part if acc is None else acc + part
    return _layer_norm(DEEPNORM_ALPHA * x + 0.5 * acc, g, b)


def _pre_kernel(x_ref, w1i_ref, w1o_ref, g1_ref, b1_ref, x1_ref, x1b_ref):
    for rows in _SUBTILES:
        x1 = _ffn_ln(x_ref[rows, :], w1i_ref, w1o_ref, g1_ref[...], b1_ref[...])
        x1_ref[rows, :] = x1
        x1b_ref[rows, :] = x1.astype(BF16)


def _resident(shape):
    return pl.BlockSpec(shape, lambda *_: (0,) * len(shape), pipeline_mode=pl.Buffered(1))


def _rows(width):
    return pl.BlockSpec((TOKEN_TILE, width), lambda i: (i, 0))


def _pre_call(x2d, w1i, w1o, g1, b1):
    n = x2d.shape[0]
    return pl.pallas_call(
        _pre_kernel,
        grid=(n // TOKEN_TILE,),
        in_specs=[_rows(D_MODEL), _resident(w1i.shape), _resident(w1o.shape),
                  _resident(g1.shape), _resident(b1.shape)],
        out_specs=[_rows(D_MODEL), _rows(D_MODEL)],
        out_shape=[jax.ShapeDtypeStruct((n, D_MODEL), F32),
                   jax.ShapeDtypeStruct((n, D_MODEL), BF16)],
        compiler_params=pltpu.CompilerParams(
            dimension_semantics=("arbitrary",), vmem_limit_bytes=V7X_VMEM_LIMIT_BYTES),
        name="pre_ffn",
    )(x2d, w1i, w1o, g1, b1)


_MIX_WIDTHS = (HG_FDIM, HG_FDIM, HG_WIDTH, HG_WIDTH, SSD_WIDTH, SSD_CONV_CH, DT_PAD)
_MIX_DTYPES = (F32, F32, BF16, F32, F32, F32, F32)


def _proj_kernel(x1b_ref, wmix_ref, *out_refs):
    c = 0
    for ref in out_refs:
        w = ref.shape[-1]
        ref[...] = _dot(x1b_ref[...], wmix_ref[:, c:c + w]).astype(ref.dtype)
        c += w


def _proj_call(x1b, wmix):
    n = x1b.shape[0]
    return pl.pallas_call(
        _proj_kernel,
        grid=(n // TOKEN_TILE,),
        in_specs=[_rows(D_MODEL), _resident(wmix.shape)],
        out_specs=[_rows(w) for w in _MIX_WIDTHS],
        out_shape=[jax.ShapeDtypeStruct((n, w), d) for w, d in zip(_MIX_WIDTHS, _MIX_DTYPES)],
        compiler_params=pltpu.CompilerParams(
            dimension_semantics=("arbitrary",), vmem_limit_bytes=V7X_VMEM_LIMIT_BYTES),
        name="mix_inproj",
    )(x1b, wmix)


def _hgrn_kernel(lbl_ref, ng_ref, suffix_ref, sumsel_ref, q_ref, f_ref, v_ref, g_ref, o_ref,
                 st_ref, f_sc, kk_sc, p_sc, a_sc, qa_sc, ke_sc, fall_sc):
    @pl.when(pl.program_id(1) == 0)
    def _():
        st_ref[...] = jnp.zeros_like(st_ref)

    logits = lbl_ref[...]
    ex = jnp.exp(logits - jnp.max(logits, axis=0, keepdims=True))
    lb = ex[0:1, :] / jnp.sum(ex, axis=0, keepdims=True)
    ng = ng_ref[...]

    row = lax.broadcasted_iota(jnp.int32, (CHUNK, CHUNK), 0)
    col = lax.broadcasted_iota(jnp.int32, (CHUNK, CHUNK), 1)
    diag_mask = (col // HALF == row // HALF) & (col <= row)
    pair_mask = col // SUB == row // SUB
    tloc = lax.broadcasted_iota(jnp.int32, (HALF, HG_DK), 0)
    zero_blk = jnp.zeros((SUB, HG_DK), BF16)

    def blk(x, i):
        return x[i * SUB:(i + 1) * SUB, :]

    def scores(c, buf):
        rows = pl.ds(pl.multiple_of(c * CHUNK, CHUNK), CHUNK)
        f = lb + (1.0 - lb) * _sigmoid(f_ref[0, rows, :])
        kk = 1.0 - f
        qq = _silu(q_ref[0, rows, :])
        f_sc[buf] = f
        kk_sc[buf] = kk
        d8 = jnp.exp2(_dot_sel(suffix_ref[...], jnp.log2(f)))
        kd8 = kk * d8
        f8 = [f[b * HALF:b * HALF + 1, :] * d8[b * HALF:b * HALF + 1, :] for b in range(2 * NSUB)]

        qp8 = []
        a_diag = []
        hsl = [slice(h * HG_DK, (h + 1) * HG_DK) for h in range(HG_HEADS)]
        for i in range(NSUB):
            q_lo = [qq[i * SUB:i * SUB + HALF, hs] for hs in hsl]
            q_hi = [qq[i * SUB + HALF:(i + 1) * SUB, hs] for hs in hsl]
            w_lo, w_hi = list(q_lo), list(q_hi)
            for s in range(HALF - 1, -1, -1):
                r_lo = i * SUB + s
                r_hi = r_lo + HALF
                keep = tloc >= s
                for h, hs in enumerate(hsl):
                    k_lo = jnp.broadcast_to(kk_sc[buf, r_lo:r_lo + 1, hs], (HALF, HG_DK))
                    k_hi = jnp.broadcast_to(kk_sc[buf, r_hi:r_hi + 1, hs], (HALF, HG_DK))
                    piece = jnp.concatenate([w_lo[h] * k_lo, w_hi[h] * k_hi], axis=0).astype(BF16)
                    p_sc[buf, i, h * SUB:(h + 1) * SUB, s * HG_DK:(s + 1) * HG_DK] = piece
                    w_lo[h] = w_lo[h] * jnp.broadcast_to(f_sc[buf, r_lo:r_lo + 1, hs], (HALF, HG_DK))
                    w_hi[h] = w_hi[h] * jnp.broadcast_to(f_sc[buf, r_hi:r_hi + 1, hs], (HALF, HG_DK))
                    if s > 0:
                        w_lo[h] = jnp.where(keep, w_lo[h], q_lo[h])
                        w_hi[h] = jnp.where(keep, w_hi[h], q_hi[h])
            qp8.append((jnp.concatenate(w_lo, axis=1), jnp.concatenate(w_hi, axis=1)))
            a_diag.append(_dot(p_sc[buf, i], sumsel_ref[...]))

        zero8 = jnp.zeros((HALF, HG_FDIM), F32)
        lhs16 = jnp.concatenate([x for i in range(NSUB) for x in (zero8, qp8[i][1])],
                                axis=0).astype(BF16)
        rhs16 = jnp.concatenate(
            [x for i in range(NSUB) for x in (kd8[i * SUB:i * SUB + HALF, :], zero8)],
            axis=0).astype(BF16)
        qp = [jnp.concatenate([qp8[i][0], qp8[i][1] * f8[2 * i]], axis=0) for i in range(NSUB)]
        kd = jnp.concatenate(
            [x for i in range(NSUB)
             for x in (kd8[i * SUB:i * SUB + HALF, :] * f8[2 * i + 1],
                       kd8[i * SUB + HALF:(i + 1) * SUB, :])], axis=0)
        fblk = [f8[2 * i] * f8[2 * i + 1] for i in range(NSUB)]

        f01 = fblk[0] * fblk[1]
        f12 = fblk[1] * fblk[2]
        f23 = fblk[2] * fblk[3]
        f012 = f01 * fblk[2]
        f123 = fblk[1] * f23
        fall_sc[buf, 0:1, :] = f01 * f23
        qa_sc[buf] = jnp.concatenate(
            [qp[0], qp[1] * fblk[0], qp[2] * f01, qp[3] * f012], axis=0).astype(BF16)
        ke_sc[buf] = jnp.concatenate(
            [blk(kd, 0) * f123, blk(kd, 1) * f23, blk(kd, 2) * fblk[3], blk(kd, 3)],
            axis=0).astype(BF16)
        qpb = [x.astype(BF16) for x in qp]
        kdb = [blk(kd, j).astype(BF16) for j in range(NSUB)]
        kd0_f1 = (blk(kd, 0) * fblk[1]).astype(BF16)
        kd0_f12 = (blk(kd, 0) * f12).astype(BF16)
        kd1_f2 = (blk(kd, 1) * fblk[2]).astype(BF16)
        for h in range(HG_HEADS):
            hs = slice(h * HG_DK, (h + 1) * HG_DK)
            z = zero_blk
            lhs = jnp.concatenate([
                jnp.concatenate([z, qpb[1][:, hs], z, z], axis=0),
                jnp.concatenate([z, z, qpb[2][:, hs], z], axis=0),
                jnp.concatenate([z, z, z, qpb[3][:, hs]], axis=0)], axis=1)
            rhs = jnp.concatenate([
                jnp.concatenate([kdb[0][:, hs], z, z, z], axis=0),
                jnp.concatenate([kd0_f1[:, hs], kdb[1][:, hs], z, z], axis=0),
                jnp.concatenate([kd0_f12[:, hs], kd1_f2[:, hs], kdb[2][:, hs], z], axis=0)],
                axis=1)
            a_diag_h = jnp.concatenate([a_diag[i][h * SUB:(h + 1) * SUB, :] for i in range(NSUB)],
                                       axis=0)
            a = (_dot_nt(lhs, rhs)
                 + jnp.where(pair_mask, _dot_nt(lhs16[:, hs], rhs16[:, hs]), 0.0)
                 + jnp.where(diag_mask, a_diag_h, 0.0))
            a_sc[buf, h] = a.astype(BF16)

    def apply(c, buf):
        rows = pl.ds(pl.multiple_of(c * CHUNK, CHUNK), CHUNK)
        f_all = fall_sc[buf, 0:1, :]
        for h in range(HG_HEADS):
            hs = slice(h * HG_DK, (h + 1) * HG_DK)
            vb = v_ref[0, rows, hs]
            st = st_ref[h]
            o = _dot(a_sc[buf, h], vb) + _dot_nt(qa_sc[buf, :, hs], st.astype(BF16))
            st_ref[h] = st * f_all[:, hs] + _dot_tn(vb, ke_sc[buf, :, hs])
            o = o * lax.rsqrt(jnp.mean(o * o, axis=-1, keepdims=True) + RMS_EPS) * ng
            o = o * _silu(g_ref[0, rows, hs])
            o_ref[0, rows, hs] = o.astype(o_ref.dtype)

    n_pairs = HGRN_TILE // CHUNK // 2
    scores(0, 0)

    def body(j, carry):
        c = 2 * j
        apply(c, 0)
        scores(c + 1, 1)
        scores(c + 2, 0)
        apply(c + 1, 1)
        return carry

    lax.fori_loop(0, n_pairs - 1, body, 0)
    last = 2 * n_pairs - 2
    apply(last, 0)
    scores(last + 1, 1)
    apply(last + 1, 1)


def _seq_spec(tile, width):
    return pl.BlockSpec((1, tile, width), lambda b, t: (b, t, 0))


def _hgrn_selectors():
    row = lax.broadcasted_iota(jnp.int32, (CHUNK, CHUNK), 0)
    col = lax.broadcasted_iota(jnp.int32, (CHUNK, CHUNK), 1)
    suffix = ((col // HALF == row // HALF) & (col > row)).astype(BF16)
    srow = lax.broadcasted_iota(jnp.int32, (HALF * HG_DK, CHUNK), 0)
    scol = lax.broadcasted_iota(jnp.int32, (HALF * HG_DK, CHUNK), 1)
    sumsel = (srow // HG_DK == scol % HALF).astype(BF16)
    return suffix, sumsel


def _hgrn_call(lb_logits, norm_g, hq, hf, hi, hg):
    bsz, seqlen, _ = hq.shape
    suffix, sumsel = _hgrn_selectors()
    return pl.pallas_call(
        _hgrn_kernel,
        grid=(bsz, seqlen // HGRN_TILE),
        in_specs=[_resident(lb_logits.shape), _resident(norm_g.shape),
                  _resident(suffix.shape), _resident(sumsel.shape),
                  _seq_spec(HGRN_TILE, HG_FDIM), _seq_spec(HGRN_TILE, HG_FDIM),
                  _seq_spec(HGRN_TILE, HG_WIDTH), _seq_spec(HGRN_TILE, HG_WIDTH)],
        out_specs=_seq_spec(HGRN_TILE, HG_WIDTH),
        out_shape=jax.ShapeDtypeStruct((bsz, seqlen, HG_WIDTH), BF16),
        scratch_shapes=[
            pltpu.VMEM((HG_HEADS, HG_DV, HG_DK), F32),
            pltpu.VMEM((2, CHUNK, HG_FDIM), F32),
            pltpu.VMEM((2, CHUNK, HG_FDIM), F32),
            pltpu.VMEM((2, NSUB, HG_HEADS * SUB, HALF * HG_DK), BF16),
            pltpu.VMEM((2, HG_HEADS, CHUNK, CHUNK), BF16),
            pltpu.VMEM((2, CHUNK, HG_FDIM), BF16),
            pltpu.VMEM((2, CHUNK, HG_FDIM), BF16),
            pltpu.VMEM((2, SUBLANES, HG_FDIM), F32),
        ],
        compiler_params=pltpu.CompilerParams(
            dimension_semantics=("arbitrary", "arbitrary"), vmem_limit_bytes=V7X_VMEM_LIMIT_BYTES),
        name="hgrn2_mixer",
    )(lb_logits, norm_g, suffix, sumsel, hq, hf, hi, hg)


def _ssd_kernel(cw_ref, cb_ref, dtb_ref, alog_ref, dexp_ref, ng_ref,
                expand_ref, tri_ref, upper_ref,
                z_ref, xbc_ref, dt_ref, o_ref, ht_ref, xbuf, xc_sc):
    @pl.when(pl.program_id(1) == 0)
    def _():
        ht_ref[...] = jnp.zeros_like(ht_ref)
        xbuf[...] = jnp.zeros_like(xbuf)

    dexp = dexp_ref[...]
    ng = ng_ref[...]
    rate = -jnp.exp(alog_ref[...]) * LOG2_E

    for r0 in range(0, SEQ_TILE, CONV_ROWS):
        if r0 == 0:
            win = jnp.concatenate([xbuf[...], xbc_ref[0, 0:CONV_ROWS, :]], axis=0)
        else:
            win = xbc_ref[0, r0 - SUBLANES:r0 + CONV_ROWS, :]
        conv = cb_ref[...] + cw_ref[SSD_CONV - 1:SSD_CONV, :] * win[SUBLANES:, :]
        for d in range(1, SSD_CONV):
            shifted = pltpu.roll(win, d, axis=0)[SUBLANES:, :]
            conv = conv + cw_ref[SSD_CONV - 1 - d:SSD_CONV - d, :] * shifted
        xc_sc[r0:r0 + CONV_ROWS, :] = _silu(conv)
    xbuf[...] = xbc_ref[0, SEQ_TILE - SUBLANES:SEQ_TILE, :]

    dt = _softplus(dt_ref[0] + dtb_ref[...])
    a = dt * rate
    expand = expand_ref[...]
    acum_h = _dot_sel(tri_ref[...], a)
    acum_all = _dot_x_sel(acum_h, expand)
    acum_t_all = _dot_x_sel(a.T, upper_ref[...])
    dtx_all = xc_sc[:, 0:SSD_WIDTH] * _dot_x_sel(dt, expand)

    row = lax.broadcasted_iota(jnp.int32, (SSD_Q, LANES), 0)
    lane = lax.broadcasted_iota(jnp.int32, (SSD_Q, LANES), 1)
    key = lane % SSD_HEADDIM
    first_head = lane < SSD_HEADDIM
    krow = lax.broadcasted_iota(jnp.int32, (SSD_HEADDIM, LANES), 1) < SSD_HEADDIM

    for c in range(SEQ_TILE // SSD_Q):
        rows = slice(c * SSD_Q, (c + 1) * SSD_Q)
        acum = acum_all[rows, :]
        acum_t = acum_t_all[:, rows]
        xs = xc_sc[rows, 0:SSD_WIDTH]
        dtx = dtx_all[rows, :]
        a_last = acum[SSD_Q - 1:SSD_Q, :]
        dtx_end = (dtx * jnp.exp2(a_last - acum)).astype(BF16)
        dtxb = dtx.astype(BF16)
        e_cum = jnp.exp2(acum)
        e_last = jnp.exp2(a_last)

        ys = []
        for g in range(SSD_GROUPS):
            gs = slice(g * SSD_GROUP_W, (g + 1) * SSD_GROUP_W)
            b0 = SSD_WIDTH + g * SSD_STATE
            c0 = SSD_WIDTH + SSD_GROUPS * SSD_STATE + g * SSD_STATE
            bm = xc_sc[rows, b0:b0 + SSD_STATE].astype(BF16)
            cm = xc_sc[rows, c0:c0 + SSD_STATE].astype(BF16)
            ht = ht_ref[g]
            y_grp = _dot(cm, ht.astype(BF16)) * e_cum[:, gs]
            y_pairs = [None] * (SSD_HPG // 2)
            for half in range(SSD_Q // SSD_HEADDIM):
                s0 = half * SSD_HEADDIM
                bm_half = bm[s0:s0 + SSD_HEADDIM, :]
                cb2 = _dot_nt(cm, jnp.concatenate([bm_half, bm_half], axis=0))
                causal2 = row >= key + s0
                for pr in range(SSD_HPG // 2):
                    h = g * SSD_HPG + 2 * pr
                    lanes = slice(h * SSD_HEADDIM, (h + 2) * SSD_HEADDIM)
                    keys_a = acum_t[h:h + 1, s0:s0 + SSD_HEADDIM]
                    keys_b = acum_t[h + 1:h + 2, s0:s0 + SSD_HEADDIM]
                    keys_t = jnp.where(first_head[0:1, :],
                                       jnp.concatenate([keys_a, keys_a], axis=1),
                                       jnp.concatenate([keys_b, keys_b], axis=1))
                    seg = acum[:, lanes] - keys_t
                    lmat = jnp.exp2(jnp.where(causal2, seg, NEG_BIG))
                    x_half = dtxb[s0:s0 + SSD_HEADDIM, lanes]
                    zero = jnp.zeros_like(x_half)
                    rhs = jnp.concatenate([jnp.where(krow, x_half, zero),
                                           jnp.where(krow, zero, x_half)], axis=0)
                    part = _dot((cb2 * lmat).astype(BF16), rhs)
                    y_pairs[pr] = part if y_pairs[pr] is None else y_pairs[pr] + part
            ys.append(jnp.concatenate(y_pairs, axis=1) + y_grp)
            ht_ref[g] = ht * e_last[:, gs] + _dot_tn(bm, dtx_end[:, gs])
        y = jnp.concatenate(ys, axis=1) + dexp * xs
        y = y * _silu(z_ref[0, rows, :])
        y = y * lax.rsqrt(jnp.mean(y * y, axis=-1, keepdims=True) + RMS_EPS) * ng
        o_ref[0, rows, :] = y.astype(o_ref.dtype)


def _ssd_selectors():
    hrow = lax.broadcasted_iota(jnp.int32, (DT_PAD, SSD_WIDTH), 0)
    hcol = lax.broadcasted_iota(jnp.int32, (DT_PAD, SSD_WIDTH), 1)
    expand = (hrow == hcol // SSD_HEADDIM).astype(BF16)
    row = lax.broadcasted_iota(jnp.int32, (SEQ_TILE, SEQ_TILE), 0)
    col = lax.broadcasted_iota(jnp.int32, (SEQ_TILE, SEQ_TILE), 1)
    same_chunk = row // SSD_Q == col // SSD_Q
    tri = (same_chunk & (col <= row)).astype(BF16)
    upper = (same_chunk & (row <= col)).astype(BF16)
    return expand, tri, upper


def _ssd_call(conv_w, conv_b, dt_bias, a_log, d_exp, norm_g, sz, sxbc, sdt):
    bsz, seqlen, _ = sz.shape
    params = (conv_w, conv_b, dt_bias, a_log, d_exp, norm_g) + _ssd_selectors()
    return pl.pallas_call(
        _ssd_kernel,
        grid=(bsz, seqlen // SEQ_TILE),
        in_specs=[_resident(p.shape) for p in params]
        + [_seq_spec(SEQ_TILE, SSD_WIDTH), _seq_spec(SEQ_TILE, SSD_CONV_CH),
           _seq_spec(SEQ_TILE, DT_PAD)],
        out_specs=_seq_spec(SEQ_TILE, SSD_WIDTH),
        out_shape=jax.ShapeDtypeStruct((bsz, seqlen, SSD_WIDTH), BF16),
        scratch_shapes=[
            pltpu.VMEM((SSD_GROUPS, SSD_STATE, SSD_GROUP_W), F32),
            pltpu.VMEM((SUBLANES, SSD_CONV_CH), F32),
            pltpu.VMEM((SEQ_TILE, SSD_CONV_CH), F32),
        ],
        compiler_params=pltpu.CompilerParams(
            dimension_semantics=("arbitrary", "arbitrary"), vmem_limit_bytes=V7X_VMEM_LIMIT_BYTES),
        name="ssd_mixer",
    )(*params, sz, sxbc, sdt)


def _post_kernel(oh_ref, os_ref, x1_ref, p_ref, wo_ref, g2_ref, b2_ref,
                 w2i_ref, w2o_ref, g3_ref, b3_ref, wg_ref, wp_ref, g4_ref, b4_ref, out_ref):
    x2 = []
    for rows in _SUBTILES:
        mix = (_dot(oh_ref[rows, :], wo_ref[0:HG_WIDTH, :])
               + _dot(os_ref[rows, :], wo_ref[HG_WIDTH:, :]))
        x2.append(_layer_norm(DEEPNORM_ALPHA * x1_ref[rows, :] + mix, g2_ref[...], b2_ref[...]))
    x3 = [_ffn_ln(x, w2i_ref, w2o_ref, g3_ref[...], b3_ref[...]) for x in x2]
    for rows, x in zip(_SUBTILES, x3):
        gate = _sigmoid(_dot(x.astype(BF16), wg_ref[...]))
        ple = gate * _dot(p_ref[rows, :].astype(BF16), wp_ref[...])
        out_ref[rows, :] = _layer_norm(DEEPNORM_ALPHA * x + ple, g4_ref[...], b4_ref[...])


def _post_call(oh, osd, x1, p2d, wo, g2, b2, w2i, w2o, g3, b3, wg, wp, g4, b4):
    n = x1.shape[0]
    weights = (wo, g2, b2, w2i, w2o, g3, b3, wg, wp, g4, b4)
    return pl.pallas_call(
        _post_kernel,
        grid=(n // TOKEN_TILE,),
        in_specs=[_rows(HG_WIDTH), _rows(SSD_WIDTH), _rows(D_MODEL), _rows(PLE_DIM)]
        + [_resident(w.shape) for w in weights],
        out_specs=_rows(D_MODEL),
        out_shape=jax.ShapeDtypeStruct((n, D_MODEL), F32),
        compiler_params=pltpu.CompilerParams(
            dimension_semantics=("arbitrary",), vmem_limit_bytes=V7X_VMEM_LIMIT_BYTES),
        name="post_outproj_ffn_ple",
    )(oh, osd, x1, p2d, *weights)


def _row(v):
    return v.reshape(1, -1).astype(F32)


def kernel(x, p, ffn1_w_in, ffn1_w_out, ln1_g, ln1_b, w_in_mix, hgrn_lb_logits, hgrn_norm_g,
           ssd_conv_w, ssd_conv_b, ssd_dt_bias, ssd_a_log, ssd_d, ssd_norm_g, w_out_mix,
           ln2_g, ln2_b, ffn2_w_in, ffn2_w_out, ln3_g, ln3_b, ple_w_proj, ple_w_gate,
           ln4_g, ln4_b):
    bsz, seqlen, _ = x.shape
    n = bsz * seqlen
    assert DEPTH == 1 and n % TOKEN_TILE == 0
    assert seqlen % SEQ_TILE == 0 and seqlen % HGRN_TILE == 0
    h = x.reshape(n, D_MODEL)
    for i in range(DEPTH):
        pad = MIX_COLS_PADDED - w_in_mix.shape[-1]
        wmix = jnp.pad(w_in_mix[i], ((0, 0), (0, pad))).astype(BF16)
        x1, x1b = _pre_call(h, ffn1_w_in[i].astype(BF16), ffn1_w_out[i].astype(BF16),
                            _row(ln1_g[i]), _row(ln1_b[i]))
        hq, hf, hi, hg, sz, sxbc, sdt = _proj_call(x1b, wmix)

        def seq(t):
            return t.reshape(bsz, seqlen, t.shape[-1])

        o_h = _hgrn_call(hgrn_lb_logits.astype(F32), _row(hgrn_norm_g[i]),
                         seq(hq), seq(hf), seq(hi), seq(hg))
        head_pad = DT_PAD - SSD_HEADS
        o_s = _ssd_call(
            ssd_conv_w[i].astype(F32), _row(ssd_conv_b[i]),
            jnp.pad(_row(ssd_dt_bias[i]), ((0, 0), (0, head_pad))),
            jnp.pad(_row(ssd_a_log[i]), ((0, 0), (0, head_pad))),
            _row(jnp.repeat(ssd_d[i], SSD_HEADDIM)), _row(ssd_norm_g[i]),
            seq(sz), seq(sxbc), seq(sdt))
        h = _post_call(
            o_h.reshape(n, HG_WIDTH), o_s.reshape(n, SSD_WIDTH), x1, p[i].reshape(n, PLE_DIM),
            w_out_mix[i].astype(BF16), _row(ln2_g[i]), _row(ln2_b[i]),
            ffn2_w_in[i].astype(BF16), ffn2_w_out[i].astype(BF16), _row(ln3_g[i]), _row(ln3_b[i]),
            ple_w_gate[i].astype(BF16), ple_w_proj[i].astype(BF16), _row(ln4_g[i]), _row(ln4_b[i]))
    return h.reshape(bsz, seqlen, D_MODEL)
```

```python
import jax
import jax.numpy as jnp
from jax import lax
from jax.experimental import pallas as pl
from jax.experimental.pallas import tpu as pltpu

F32 = jnp.float32
BF16 = jnp.bfloat16

DEPTH = 1
D_MODEL = 1024
D_FF = 2816
CHUNK = 64
PLE_DIM = 256
HG_HEADS = 4
HG_DK = 128
HG_DV = 128
HG_WIDTH = HG_HEADS * HG_DV
HG_FDIM = HG_HEADS * HG_DK
SSD_WIDTH = 512
SSD_HEADDIM = 64
SSD_HEADS = SSD_WIDTH // SSD_HEADDIM
SSD_GROUPS = 2
SSD_HPG = SSD_HEADS // SSD_GROUPS
SSD_STATE = 128
SSD_CONV = 4
SSD_CONV_CH = SSD_WIDTH + 2 * SSD_GROUPS * SSD_STATE
SSD_GROUP_W = SSD_HPG * SSD_HEADDIM
DEEPNORM_ALPHA = (2.0 * DEPTH) ** 0.25
LN_EPS = 1e-5
RMS_EPS = 1e-6
LOG2_E = 1.4426950408889634

LANES = 128
SUBLANES = 8
V7X_MXU_DIM = 256
V7X_VMEM_LIMIT_BYTES = 56 * 1024 * 1024

DT_PAD = LANES
MIX_COLS_PADDED = 2 * HG_FDIM + 2 * HG_WIDTH + SSD_WIDTH + SSD_CONV_CH + DT_PAD
TOKEN_TILE = 1024
SUBTILE = 256
_SUBTILES = tuple(slice(r, r + SUBTILE) for r in range(0, TOKEN_TILE, SUBTILE))
FF_SPLITS = (0, (D_FF // V7X_MXU_DIM + 1) // 2 * V7X_MXU_DIM, D_FF)
SEQ_TILE = 256
HGRN_TILE = 2048
SUB = 16
NSUB = CHUNK // SUB
HALF = SUB // 2
NEG_BIG = -1e30
SSD_Q = 128
CONV_ROWS = 64


def _sigmoid(x):
    return 1.0 / (1.0 + jnp.exp(-x))


def _silu(x):
    return x * _sigmoid(x)


def _softplus(x):
    return jnp.maximum(x, 0.0) + jnp.log1p(jnp.exp(-jnp.abs(x)))


def _dot(a, b):
    return jnp.dot(a, b, preferred_element_type=F32)


def _dot_nt(a, b):
    return lax.dot_general(a, b, (((1,), (1,)), ((), ())), preferred_element_type=F32)


def _dot_tn(a, b):
    return lax.dot_general(a, b, (((0,), (0,)), ((), ())), preferred_element_type=F32)


def _split3(x):
    hi = x.astype(BF16)
    r1 = x - hi.astype(F32)
    mid = r1.astype(BF16)
    lo = (r1 - mid.astype(F32)).astype(BF16)
    return hi, mid, lo


def _dot_sel(sel, x):
    hi, mid, lo = _split3(x)
    return _dot(sel, hi) + _dot(sel, mid) + _dot(sel, lo)


def _dot_x_sel(x, sel):
    hi, mid, lo = _split3(x)
    return _dot(hi, sel) + _dot(mid, sel) + _dot(lo, sel)


def _layer_norm(y, g, b):
    mu = jnp.mean(y, axis=-1, keepdims=True)
    d = y - mu
    var = jnp.mean(d * d, axis=-1, keepdims=True)
    return d * lax.rsqrt(var + LN_EPS) * g + b


def _ffn_ln(x, w_in_ref, w_out_ref, g, b):
    xb = x.astype(BF16)
    acc = None
    for c0, c1 in zip(FF_SPLITS[:-1], FF_SPLITS[1:]):
        gate = _dot(xb, w_in_ref[:, c0:c1])
        up = _dot(xb, w_in_ref[:, D_FF + c0:D_FF + c1])
        act = (_silu(gate) * up).astype(BF16)
        part = _dot(act, w_out_ref[c0:c1, :])
        acc = part if acc is None else acc + part
    return _layer_norm(DEEPNORM_ALPHA * x + 0.5 * acc, g, b)


def _pre_kernel(x_ref, w1i_ref, w1o_ref, g1_ref, b1_ref, x1_ref, x1b_ref):
    for rows in _SUBTILES:
        x1 = _ffn_ln(x_ref[rows, :], w1i_ref, w1o_ref, g1_ref[...], b1_ref[...])
        x1_ref[rows, :] = x1
        x1b_ref[rows, :] = x1.astype(BF16)


def _resident(shape):
    return pl.BlockSpec(shape, lambda *_: (0,) * len(shape), pipeline_mode=pl.Buffered(1))


def _rows(width):
    return pl.BlockSpec((TOKEN_TILE, width), lambda i: (i, 0))


def _pre_call(x2d, w1i, w1o, g1, b1):
    n = x2d.shape[0]
    return pl.pallas_call(
        _pre_kernel,
        grid=(n // TOKEN_TILE,),
        in_specs=[_rows(D_MODEL), _resident(w1i.shape), _resident(w1o.shape),
                  _resident(g1.shape), _resident(b1.shape)],
        out_specs=[_rows(D_MODEL), _rows(D_MODEL)],
        out_shape=[jax.ShapeDtypeStruct((n, D_MODEL), F32),
                   jax.ShapeDtypeStruct((n, D_MODEL), BF16)],
        compiler_params=pltpu.CompilerParams(
            dimension_semantics=("arbitrary",), vmem_limit_bytes=V7X_VMEM_LIMIT_BYTES),
        name="pre_ffn",
    )(x2d, w1i, w1o, g1, b1)


_MIX_WIDTHS = (HG_FDIM, HG_FDIM, HG_WIDTH, HG_WIDTH, SSD_WIDTH, SSD_CONV_CH, DT_PAD)
_MIX_DTYPES = (F32, F32, BF16, F32, F32, F32, F32)


def _proj_kernel(x1b_ref, wmix_ref, *out_refs):
    c = 0
    for ref in out_refs:
        w = ref.shape[-1]
        ref[...] = _dot(x1b_ref[...], wmix_ref[:, c:c + w]).astype(ref.dtype)
        c += w


def _proj_call(x1b, wmix):
    n = x1b.shape[0]
    return pl.pallas_call(
        _proj_kernel,
        grid=(n // TOKEN_TILE,),
        in_specs=[_rows(D_MODEL), _resident(wmix.shape)],
        out_specs=[_rows(w) for w in _MIX_WIDTHS],
        out_shape=[jax.ShapeDtypeStruct((n, w), d) for w, d in zip(_MIX_WIDTHS, _MIX_DTYPES)],
        compiler_params=pltpu.CompilerParams(
            dimension_semantics=("arbitrary",), vmem_limit_bytes=V7X_VMEM_LIMIT_BYTES),
        name="mix_inproj",
    )(x1b, wmix)


def _hgrn_kernel(lbl_ref, ng_ref, suffix_ref, sumsel_ref, q_ref, f_ref, v_ref, g_ref, o_ref,
                 st_ref, f_sc, kk_sc, p_sc, a_sc, qa_sc, ke_sc, fall_sc):
    @pl.when(pl.program_id(1) == 0)
    def _():
        st_ref[...] = jnp.zeros_like(st_ref)

    logits = lbl_ref[...]
    ex = jnp.exp(logits - jnp.max(logits, axis=0, keepdims=True))
    lb = ex[0:1, :] / jnp.sum(ex, axis=0, keepdims=True)
    ng = ng_ref[...]

    row = lax.broadcasted_iota(jnp.int32, (CHUNK, CHUNK), 0)
    col = lax.broadcasted_iota(jnp.int32, (CHUNK, CHUNK), 1)
    diag_mask = (col // HALF == row // HALF) & (col <= row)
    pair_mask = col // SUB == row // SUB
    tloc = lax.broadcasted_iota(jnp.int32, (HALF, HG_DK), 0)
    zero_blk = jnp.zeros((SUB, HG_DK), BF16)

    def blk(x, i):
        return x[i * SUB:(i + 1) * SUB, :]

    def scores(c, buf):
        rows = pl.ds(pl.multiple_of(c * CHUNK, CHUNK), CHUNK)
        f = lb + (1.0 - lb) * _sigmoid(f_ref[0, rows, :])
        kk = 1.0 - f
        qq = _silu(q_ref[0, rows, :])
        f_sc[buf] = f
        kk_sc[buf] = kk
        d8 = jnp.exp2(_dot_sel(suffix_ref[...], jnp.log2(f)))
        kd8 = kk * d8
        f8 = [f[b * HALF:b * HALF + 1, :] * d8[b * HALF:b * HALF + 1, :] for b in range(2 * NSUB)]

        qp8 = []
        a_diag = []
        hsl = [slice(h * HG_DK, (h + 1) * HG_DK) for h in range(HG_HEADS)]
        for i in range(NSUB):
            q_lo = [qq[i * SUB:i * SUB + HALF, hs] for hs in hsl]
            q_hi = [qq[i * SUB + HALF:(i + 1) * SUB, hs] for hs in hsl]
            w_lo, w_hi = list(q_lo), list(q_hi)
            for s in range(HALF - 1, -1, -1):
                r_lo = i * SUB + s
                r_hi = r_lo + HALF
                keep = tloc >= s
                for h, hs in enumerate(hsl):
                    k_lo = jnp.broadcast_to(kk_sc[buf, r_lo:r_lo + 1, hs], (HALF, HG_DK))
                    k_hi = jnp.broadcast_to(kk_sc[buf, r_hi:r_hi + 1, hs], (HALF, HG_DK))
                    piece = jnp.concatenate([w_lo[h] * k_lo, w_hi[h] * k_hi], axis=0).astype(BF16)
                    p_sc[buf, i, h * SUB:(h + 1) * SUB, s * HG_DK:(s + 1) * HG_DK] = piece
                    w_lo[h] = w_lo[h] * jnp.broadcast_to(f_sc[buf, r_lo:r_lo + 1, hs], (HALF, HG_DK))
                    w_hi[h] = w_hi[h] * jnp.broadcast_to(f_sc[buf, r_hi:r_hi + 1, hs], (HALF, HG_DK))
                    if s > 0:
                        w_lo[h] = jnp.where(keep, w_lo[h], q_lo[h])
                        w_hi[h] = jnp.where(keep, w_hi[h], q_hi[h])
            qp8.append((jnp.concatenate(w_lo, axis=1), jnp.concatenate(w_hi, axis=1)))
            a_diag.append(_dot(p_sc[buf, i], sumsel_ref[...]))

        zero8 = jnp.zeros((HALF, HG_FDIM), F32)
        lhs16 = jnp.concatenate([x for i in range(NSUB) for x in (zero8, qp8[i][1])],
                                axis=0).astype(BF16)
        rhs16 = jnp.concatenate(
            [x for i in range(NSUB) for x in (kd8[i * SUB:i * SUB + HALF, :], zero8)],
            axis=0).astype(BF16)
        qp = [jnp.concatenate([qp8[i][0], qp8[i][1] * f8[2 * i]], axis=0) for i in range(NSUB)]
        kd = jnp.concatenate(
            [x for i in range(NSUB)
             for x in (kd8[i * SUB:i * SUB + HALF, :] * f8[2 * i + 1],
                       kd8[i * SUB + HALF:(i + 1) * SUB, :])], axis=0)
        fblk = [f8[2 * i] * f8[2 * i + 1] for i in range(NSUB)]

        f01 = fblk[0] * fblk[1]
        f12 = fblk[1] * fblk[2]
        f23 = fblk[2] * fblk[3]
        f012 = f01 * fblk[2]
        f123 = fblk[1] * f23
        fall_sc[buf, 0:1, :] = f01 * f23
        qa_sc[buf] = jnp.concatenate(
            [qp[0], qp[1] * fblk[0], qp[2] * f01, qp[3] * f012], axis=0).astype(BF16)
        ke_sc[buf] = jnp.concatenate(
            [blk(kd, 0) * f123, blk(kd, 1) * f23, blk(kd, 2) * fblk[3], blk(kd, 3)],
            axis=0).astype(BF16)
        qpb = [x.astype(BF16) for x in qp]
        kdb = [blk(kd, j).astype(BF16) for j in range(NSUB)]
        kd0_f1 = (blk(kd, 0) * fblk[1]).astype(BF16)
        kd0_f12 = (blk(kd, 0) * f12).astype(BF16)
        kd1_f2 = (blk(kd, 1) * fblk[2]).astype(BF16)
        for h in range(HG_HEADS):
            hs = slice(h * HG_DK, (h + 1) * HG_DK)
            z = zero_blk
            lhs = jnp.concatenate([
                jnp.concatenate([z, qpb[1][:, hs], z, z], axis=0),
                jnp.concatenate([z, z, qpb[2][:, hs], z], axis=0),
                jnp.concatenate([z, z, z, qpb[3][:, hs]], axis=0)], axis=1)
            rhs = jnp.concatenate([
                jnp.concatenate([kdb[0][:, hs], z, z, z], axis=0),
                jnp.concatenate([kd0_f1[:, hs], kdb[1][:, hs], z, z], axis=0),
                jnp.concatenate([kd0_f12[:, hs], kd1_f2[:, hs], kdb[2][:, hs], z], axis=0)],
                axis=1)
            a_diag_h = jnp.concatenate([a_diag[i][h * SUB:(h + 1) * SUB, :] for i in range(NSUB)],
                                       axis=0)
            a = (_dot_nt(lhs, rhs)
                 + jnp.where(pair_mask, _dot_nt(lhs16[:, hs], rhs16[:, hs]), 0.0)
                 + jnp.where(diag_mask, a_diag_h, 0.0))
            a_sc[buf, h] = a.astype(BF16)

    def apply(c, buf):
        rows = pl.ds(pl.multiple_of(c * CHUNK, CHUNK), CHUNK)
        f_all = fall_sc[buf, 0:1, :]
        for h in range(HG_HEADS):
            hs = slice(h * HG_DK, (h + 1) * HG_DK)
            vb = v_ref[0, rows, hs]
            st = st_ref[h]
            o = _dot(a_sc[buf, h], vb) + _dot_nt(qa_sc[buf, :, hs], st.astype(BF16))
            st_ref[h] = st * f_all[:, hs] + _dot_tn(vb, ke_sc[buf, :, hs])
            o = o * lax.rsqrt(jnp.mean(o * o, axis=-1, keepdims=True) + RMS_EPS) * ng
            o = o * _silu(g_ref[0, rows, hs])
            o_ref[0, rows, hs] = o.astype(o_ref.dtype)

    n_pairs = HGRN_TILE // CHUNK // 2
    scores(0, 0)

    def body(j, carry):
        c = 2 * j
        apply(c, 0)
        scores(c + 1, 1)
        scores(c + 2, 0)
        apply(c + 1, 1)
        return carry

    lax.fori_loop(0, n_pairs - 1, body, 0)
    last = 2 * n_pairs - 2
    apply(last, 0)
    scores(last + 1, 1)
    apply(last + 1, 1)


def _seq_spec(tile, width):
    return pl.BlockSpec((1, tile, width), lambda b, t: (b, t, 0))


def _hgrn_selectors():
    row = lax.broadcasted_iota(jnp.int32, (CHUNK, CHUNK), 0)
    col = lax.broadcasted_iota(jnp.int32, (CHUNK, CHUNK), 1)
    suffix = ((col // HALF == row // HALF) & (col > row)).astype(BF16)
    srow = lax.broadcasted_iota(jnp.int32, (HALF * HG_DK, CHUNK), 0)
    scol = lax.broadcasted_iota(jnp.int32, (HALF * HG_DK, CHUNK), 1)
    sumsel = (srow // HG_DK == scol % HALF).astype(BF16)
    return suffix, sumsel


def _hgrn_call(lb_logits, norm_g, hq, hf, hi, hg):
    bsz, seqlen, _ = hq.shape
    suffix, sumsel = _hgrn_selectors()
    return pl.pallas_call(
        _hgrn_kernel,
        grid=(bsz, seqlen // HGRN_TILE),
        in_specs=[_resident(lb_logits.shape), _resident(norm_g.shape),
                  _resident(suffix.shape), _resident(sumsel.shape),
                  _seq_spec(HGRN_TILE, HG_FDIM), _seq_spec(HGRN_TILE, HG_FDIM),
                  _seq_spec(HGRN_TILE, HG_WIDTH), _seq_spec(HGRN_TILE, HG_WIDTH)],
        out_specs=_seq_spec(HGRN_TILE, HG_WIDTH),
        out_shape=jax.ShapeDtypeStruct((bsz, seqlen, HG_WIDTH), BF16),
        scratch_shapes=[
            pltpu.VMEM((HG_HEADS, HG_DV, HG_DK), F32),
            pltpu.VMEM((2, CHUNK, HG_FDIM), F32),
            pltpu.VMEM((2, CHUNK, HG_FDIM), F32),
            pltpu.VMEM((2, NSUB, HG_HEADS * SUB, HALF * HG_DK), BF16),
            pltpu.VMEM((2, HG_HEADS, CHUNK, CHUNK), BF16),
            pltpu.VMEM((2, CHUNK, HG_FDIM), BF16),
            pltpu.VMEM((2, CHUNK, HG_FDIM), BF16),
            pltpu.VMEM((2, SUBLANES, HG_FDIM), F32),
        ],
        compiler_params=pltpu.CompilerParams(
            dimension_semantics=("arbitrary", "arbitrary"), vmem_limit_bytes=V7X_VMEM_LIMIT_BYTES),
        name="hgrn2_mixer",
    )(lb_logits, norm_g, suffix, sumsel, hq, hf, hi, hg)


def _ssd_kernel(cw_ref, cb_ref, dtb_ref, alog_ref, dexp_ref, ng_ref,
                expand_ref, tri_ref, upper_ref,
                z_ref, xbc_ref, dt_ref, o_ref, ht_ref, xbuf, xc_sc):
    @pl.when(pl.program_id(1) == 0)
    def _():
        ht_ref[...] = jnp.zeros_like(ht_ref)
        xbuf[...] = jnp.zeros_like(xbuf)

    dexp = dexp_ref[...]
    ng = ng_ref[...]
    rate = -jnp.exp(alog_ref[...]) * LOG2_E

    for r0 in range(0, SEQ_TILE, CONV_ROWS):
        if r0 == 0:
            win = jnp.concatenate([xbuf[...], xbc_ref[0, 0:CONV_ROWS, :]], axis=0)
        else:
            win = xbc_ref[0, r0 - SUBLANES:r0 + CONV_ROWS, :]
        conv = cb_ref[...] + cw_ref[SSD_CONV - 1:SSD_CONV, :] * win[SUBLANES:, :]
        for d in range(1, SSD_CONV):
            shifted = pltpu.roll(win, d, axis=0)[SUBLANES:, :]
            conv = conv + cw_ref[SSD_CONV - 1 - d:SSD_CONV - d, :] * shifted
        xc_sc[r0:r0 + CONV_ROWS, :] = _silu(conv)
    xbuf[...] = xbc_ref[0, SEQ_TILE - SUBLANES:SEQ_TILE, :]

    dt = _softplus(dt_ref[0] + dtb_ref[...])
    a = dt * rate
    expand = expand_ref[...]
    acum_h = _dot_sel(tri_ref[...], a)
    acum_all = _dot_x_sel(acum_h, expand)
    acum_t_all = _dot_x_sel(a.T, upper_ref[...])
    dtx_all = xc_sc[:, 0:SSD_WIDTH] * _dot_x_sel(dt, expand)

    row = lax.broadcasted_iota(jnp.int32, (SSD_Q, LANES), 0)
    lane = lax.broadcasted_iota(jnp.int32, (SSD_Q, LANES), 1)
    key = lane % SSD_HEADDIM
    first_head = lane < SSD_HEADDIM
    krow = lax.broadcasted_iota(jnp.int32, (SSD_HEADDIM, LANES), 1) < SSD_HEADDIM

    for c in range(SEQ_TILE // SSD_Q):
        rows = slice(c * SSD_Q, (c + 1) * SSD_Q)
        acum = acum_all[rows, :]
        acum_t = acum_t_all[:, rows]
        xs = xc_sc[rows, 0:SSD_WIDTH]
        dtx = dtx_all[rows, :]
        a_last = acum[SSD_Q - 1:SSD_Q, :]
        dtx_end = (dtx * jnp.exp2(a_last - acum)).astype(BF16)
        dtxb = dtx.astype(BF16)
        e_cum = jnp.exp2(acum)
        e_last = jnp.exp2(a_last)

        ys = []
        for g in range(SSD_GROUPS):
            gs = slice(g * SSD_GROUP_W, (g + 1) * SSD_GROUP_W)
            b0 = SSD_WIDTH + g * SSD_STATE
            c0 = SSD_WIDTH + SSD_GROUPS * SSD_STATE + g * SSD_STATE
            bm = xc_sc[rows, b0:b0 + SSD_STATE].astype(BF16)
            cm = xc_sc[rows, c0:c0 + SSD_STATE].astype(BF16)
            ht = ht_ref[g]
            y_grp = _dot(cm, ht.astype(BF16)) * e_cum[:, gs]
            y_pairs = [None] * (SSD_HPG // 2)
            for half in range(SSD_Q // SSD_HEADDIM):
                s0 = half * SSD_HEADDIM
                bm_half = bm[s0:s0 + SSD_HEADDIM, :]
                cb2 = _dot_nt(cm, jnp.concatenate([bm_half, bm_half], axis=0))
                causal2 = row >= key + s0
                for pr in range(SSD_HPG // 2):
                    h = g * SSD_HPG + 2 * pr
                    lanes = slice(h * SSD_HEADDIM, (h + 2) * SSD_HEADDIM)
                    keys_a = acum_t[h:h + 1, s0:s0 + SSD_HEADDIM]
                    keys_b = acum_t[h + 1:h + 2, s0:s0 + SSD_HEADDIM]
                    keys_t = jnp.where(first_head[0:1, :],
                                       jnp.concatenate([keys_a, keys_a], axis=1),
                                       jnp.concatenate([keys_b, keys_b], axis=1))
                    seg = acum[:, lanes] - keys_t
                    lmat = jnp.exp2(jnp.where(causal2, seg, NEG_BIG))
                    x_half = dtxb[s0:s0 + SSD_HEADDIM, lanes]
                    zero = jnp.zeros_like(x_half)
                    rhs = jnp.concatenate([jnp.where(krow, x_half, zero),
                                           jnp.where(krow, zero, x_half)], axis=0)
                    part = _dot((cb2 * lmat).astype(BF16), rhs)
                    y_pairs[pr] = part if y_pairs[pr] is None else y_pairs[pr] + part
            ys.append(jnp.concatenate(y_pairs, axis=1) + y_grp)
            ht_ref[g] = ht * e_last[:, gs] + _dot_tn(bm, dtx_end[:, gs])
        y = jnp.concatenate(ys, axis=1) + dexp * xs
        y = y * _silu(z_ref[0, rows, :])
        y = y * lax.rsqrt(jnp.mean(y * y, axis=-1, keepdims=True) + RMS_EPS) * ng
        o_ref[0, rows, :] = y.astype(o_ref.dtype)


def _ssd_selectors():
    hrow = lax.broadcasted_iota(jnp.int32, (DT_PAD, SSD_WIDTH), 0)
    hcol = lax.broadcasted_iota(jnp.int32, (DT_PAD, SSD_WIDTH), 1)
    expand = (hrow == hcol // SSD_HEADDIM).astype(BF16)
    row = lax.broadcasted_iota(jnp.int32, (SEQ_TILE, SEQ_TILE), 0)
    col = lax.broadcasted_iota(jnp.int32, (SEQ_TILE, SEQ_TILE), 1)
    same_chunk = row // SSD_Q == col // SSD_Q
    tri = (same_chunk & (col <= row)).astype(BF16)
    upper = (same_chunk & (row <= col)).astype(BF16)
    return expand, tri, upper


def _ssd_call(conv_w, conv_b, dt_bias, a_log, d_exp, norm_g, sz, sxbc, sdt):
    bsz, seqlen, _ = sz.shape
    params = (conv_w, conv_b, dt_bias, a_log, d_exp, norm_g) + _ssd_selectors()
    return pl.pallas_call(
        _ssd_kernel,
        grid=(bsz, seqlen // SEQ_TILE),
        in_specs=[_resident(p.shape) for p in params]
        + [_seq_spec(SEQ_TILE, SSD_WIDTH), _seq_spec(SEQ_TILE, SSD_CONV_CH),
           _seq_spec(SEQ_TILE, DT_PAD)],
        out_specs=_seq_spec(SEQ_TILE, SSD_WIDTH),
        out_shape=jax.ShapeDtypeStruct((bsz, seqlen, SSD_WIDTH), BF16),
        scratch_shapes=[
            pltpu.VMEM((SSD_GROUPS, SSD_STATE, SSD_GROUP_W), F32),
            pltpu.VMEM((SUBLANES, SSD_CONV_CH), F32),
            pltpu.VMEM((SEQ_TILE, SSD_CONV_CH), F32),
        ],
        compiler_params=pltpu.CompilerParams(
            dimension_semantics=("arbitrary", "arbitrary"), vmem_limit_bytes=V7X_VMEM_LIMIT_BYTES),
        name="ssd_mixer",
    )(*params, sz, sxbc, sdt)


def _post_kernel(oh_ref, os_ref, x1_ref, p_ref, wo_ref, g2_ref, b2_ref,
                 w2i_ref, w2o_ref, g3_ref, b3_ref, wg_ref, wp_ref, g4_ref, b4_ref, out_ref):
    x2 = []
    for rows in _SUBTILES:
        mix = (_dot(oh_ref[rows, :], wo_ref[0:HG_WIDTH, :])
               + _dot(os_ref[rows, :], wo_ref[HG_WIDTH:, :]))
        x2.append(_layer_norm(DEEPNORM_ALPHA * x1_ref[rows, :] + mix, g2_ref[...], b2_ref[...]))
    x3 = [_ffn_ln(x, w2i_ref, w2o_ref, g3_ref[...], b3_ref[...]) for x in x2]
    for rows, x in zip(_SUBTILES, x3):
        gate = _sigmoid(_dot(x.astype(BF16), wg_ref[...]))
        ple = gate * _dot(p_ref[rows, :].astype(BF16), wp_ref[...])
        out_ref[rows, :] = _layer_norm(DEEPNORM_ALPHA * x + ple, g4_ref[...], b4_ref[...])


def _post_call(oh, osd, x1, p2d, wo, g2, b2, w2i, w2o, g3, b3, wg, wp, g4, b4):
    n = x1.shape[0]
    weights = (wo, g2, b2, w2i, w2o, g3, b3, wg, wp, g4, b4)
    return pl.pallas_call(
        _post_kernel,
        grid=(n // TOKEN_TILE,),
        in_specs=[_rows(HG_WIDTH), _rows(SSD_WIDTH), _rows(D_MODEL), _rows(PLE_DIM)]
        + [_resident(w.shape) for w in weights],
        out_specs=_rows(D_MODEL),
        out_shape=jax.ShapeDtypeStruct((n, D_MODEL), F32),
        compiler_params=pltpu.CompilerParams(
            dimension_semantics=("arbitrary",), vmem_limit_bytes=V7X_VMEM_LIMIT_BYTES),
        name="post_outproj_ffn_ple",
    )(oh, osd, x1, p2d, *weights)


def _row(v):
    return v.reshape(1, -1).astype(F32)


def kernel(x, p, ffn1_w_in, ffn1_w_out, ln1_g, ln1_b, w_in_mix, hgrn_lb_logits, hgrn_norm_g,
           ssd_conv_w, ssd_conv_b, ssd_dt_bias, ssd_a_log, ssd_d, ssd_norm_g, w_out_mix,
           ln2_g, ln2_b, ffn2_w_in, ffn2_w_out, ln3_g, ln3_b, ple_w_proj, ple_w_gate,
           ln4_g, ln4_b):
    bsz, seqlen, _ = x.shape
    n = bsz * seqlen
    assert DEPTH == 1 and n % TOKEN_TILE == 0
    assert seqlen % SEQ_TILE == 0 and seqlen % HGRN_TILE == 0
    h = x.reshape(n, D_MODEL)
    for i in range(DEPTH):
        pad = MIX_COLS_PADDED - w_in_mix.shape[-1]
        wmix = jnp.pad(w_in_mix[i], ((0, 0), (0, pad))).astype(BF16)
        x1, x1b = _pre_call(h, ffn1_w_in[i].astype(BF16), ffn1_w_out[i].astype(BF16),
                            _row(ln1_g[i]), _row(ln1_b[i]))
        hq, hf, hi, hg, sz, sxbc, sdt = _proj_call(x1b, wmix)

        def seq(t):
            return t.reshape(bsz, seqlen, t.shape[-1])

        o_h = _hgrn_call(hgrn_lb_logits.astype(F32), _row(hgrn_norm_g[i]),
                         seq(hq), seq(hf), seq(hi), seq(hg))
        head_pad = DT_PAD - SSD_HEADS
        o_s = _ssd_call(
            ssd_conv_w[i].astype(F32), _row(ssd_conv_b[i]),
            jnp.pad(_row(ssd_dt_bias[i]), ((0, 0), (0, head_pad))),
            jnp.pad(_row(ssd_a_log[i]), ((0, 0), (0, head_pad))),
            _row(jnp.repeat(ssd_d[i], SSD_HEADDIM)), _row(ssd_norm_g[i]),
            seq(sz), seq(sxbc), seq(sdt))
        h = _post_call(
            o_h.reshape(n, HG_WIDTH), o_s.reshape(n, SSD_WIDTH), x1, p[i].reshape(n, PLE_DIM),
            w_out_mix[i].astype(BF16), _row(ln2_g[i]), _row(ln2_b[i]),
            ffn2_w_in[i].astype(BF16), ffn2_w_out[i].astype(BF16), _row(ln3_g[i]), _row(ln3_b[i]),
            ple_w_gate[i].astype(BF16), ple_w_proj[i].astype(BF16), _row(ln4_g[i]), _row(ln4_b[i]))
    return h.reshape(bsz, seqlen, D_MODEL)
```

```python
import jax
import jax.numpy as jnp
from jax import lax
from jax.experimental import pallas as pl
from jax.experimental.pallas import tpu as pltpu

F32 = jnp.float32
BF16 = jnp.bfloat16

DEPTH = 1
D_MODEL = 1024
D_FF = 2816
CHUNK = 64
PLE_DIM = 256
HG_HEADS = 4
HG_DK = 128
HG_DV = 128
HG_WIDTH = HG_HEADS * HG_DV
HG_FDIM = HG_HEADS * HG_DK
SSD_WIDTH = 512
SSD_HEADDIM = 64
SSD_HEADS = SSD_WIDTH // SSD_HEADDIM
SSD_GROUPS = 2
SSD_HPG = SSD_HEADS // SSD_GROUPS
SSD_STATE = 128
SSD_CONV = 4
SSD_CONV_CH = SSD_WIDTH + 2 * SSD_GROUPS * SSD_STATE
SSD_GROUP_W = SSD_HPG * SSD_HEADDIM
DEEPNORM_ALPHA = (2.0 * DEPTH) ** 0.25
LN_EPS = 1e-5
RMS_EPS = 1e-6
LOG2_E = 1.4426950408889634

LANES = 128
SUBLANES = 8
V7X_MXU_DIM = 256
V7X_VMEM_LIMIT_BYTES = 56 * 1024 * 1024

DT_PAD = LANES
MIX_COLS_PADDED = 2 * HG_FDIM + 2 * HG_WIDTH + SSD_WIDTH + SSD_CONV_CH + DT_PAD
TOKEN_TILE = 1024
SUBTILE = 256
_SUBTILES = tuple(slice(r, r + SUBTILE) for r in range(0, TOKEN_TILE, SUBTILE))
FF_SPLITS = (0, (D_FF // V7X_MXU_DIM + 1) // 2 * V7X_MXU_DIM, D_FF)
SEQ_TILE = 256
HGRN_TILE = 2048
SUB = 16
NSUB = CHUNK // SUB
HALF = SUB // 2
NEG_BIG = -1e30
SSD_Q = 128
CONV_ROWS = 64


def _sigmoid(x):
    return 1.0 / (1.0 + jnp.exp(-x))


def _silu(x):
    return x * _sigmoid(x)


def _softplus(x):
    return jnp.maximum(x, 0.0) + jnp.log1p(jnp.exp(-jnp.abs(x)))


def _dot(a, b):
    return jnp.dot(a, b, preferred_element_type=F32)


def _dot_nt(a, b):
    return lax.dot_general(a, b, (((1,), (1,)), ((), ())), preferred_element_type=F32)


def _dot_tn(a, b):
    return lax.dot_general(a, b, (((0,), (0,)), ((), ())), preferred_element_type=F32)


def _split3(x):
    hi = x.astype(BF16)
    r1 = x - hi.astype(F32)
    mid = r1.astype(BF16)
    lo = (r1 - mid.astype(F32)).astype(BF16)
    return hi, mid, lo


def _dot_sel(sel, x):
    hi, mid, lo = _split3(x)
    return _dot(sel, hi) + _dot(sel, mid) + _dot(sel, lo)


def _dot_x_sel(x, sel):
    hi, mid, lo = _split3(x)
    return _dot(hi, sel) + _dot(mid, sel) + _dot(lo, sel)


def _layer_norm(y, g, b):
    mu = jnp.mean(y, axis=-1, keepdims=True)
    d = y - mu
    var = jnp.mean(d * d, axis=-1, keepdims=True)
    return d * lax.rsqrt(var + LN_EPS) * g + b


def _ffn_ln(x, w_in_ref, w_out_ref, g, b):
    xb = x.astype(BF16)
    acc = None
    for c0, c1 in zip(FF_SPLITS[:-1], FF_SPLITS[1:]):
        gate = _dot(xb, w_in_ref[:, c0:c1])
        up = _dot(xb, w_in_ref[:, D_FF + c0:D_FF + c1])
        act = (_silu(gate) * up).astype(BF16)
        part = _dot(act, w_out_ref[c0:c1, :])
        acc = part if acc is None else acc + part
    return _layer_norm(DEEPNORM_ALPHA * x + 0.5 * acc, g, b)


def _pre_kernel(x_ref, w1i_ref, w1o_ref, g1_ref, b1_ref, x1_ref, x1b_ref):
    for rows in _SUBTILES:
        x1 = _ffn_ln(x_ref[rows, :], w1i_ref, w1o_ref, g1_ref[...], b1_ref[...])
        x1_ref[rows, :] = x1
        x1b_ref[rows, :] = x1.astype(BF16)


def _resident(shape):
    return pl.BlockSpec(shape, lambda *_: (0,) * len(shape), pipeline_mode=pl.Buffered(1))


def _rows(width):
    return pl.BlockSpec((TOKEN_TILE, width), lambda i: (i, 0))


def _pre_call(x2d, w1i, w1o, g1, b1):
    n = x2d.shape[0]
    return pl.pallas_call(
        _pre_kernel,
        grid=(n // TOKEN_TILE,),
        in_specs=[_rows(D_MODEL), _resident(w1i.shape), _resident(w1o.shape),
                  _resident(g1.shape), _resident(b1.shape)],
        out_specs=[_rows(D_MODEL), _rows(D_MODEL)],
        out_shape=[jax.ShapeDtypeStruct((n, D_MODEL), F32),
                   jax.ShapeDtypeStruct((n, D_MODEL), BF16)],
        compiler_params=pltpu.CompilerParams(
            dimension_semantics=("arbitrary",), vmem_limit_bytes=V7X_VMEM_LIMIT_BYTES),
        name="pre_ffn",
    )(x2d, w1i, w1o, g1, b1)


_MIX_WIDTHS = (HG_FDIM, HG_FDIM, HG_WIDTH, HG_WIDTH, SSD_WIDTH, SSD_CONV_CH, DT_PAD)
_MIX_DTYPES = (F32, F32, BF16, F32, F32, F32, F32)


def _proj_kernel(x1b_ref, wmain_ref, wdt_ref, *out_refs):
    c = 0
    for ref in out_refs[:-1]:
        w = ref.shape[-1]
        ref[...] = _dot(x1b_ref[...], wmain_ref[:, c:c + w]).astype(ref.dtype)
        c += w
    out_refs[-1][...] = _dot(x1b_ref[...], wdt_ref[...])


def _proj_call(x1b, wmain, wdt):
    n = x1b.shape[0]
    return pl.pallas_call(
        _proj_kernel,
        grid=(n // TOKEN_TILE,),
        in_specs=[_rows(D_MODEL), _resident(wmain.shape), _resident(wdt.shape)],
        out_specs=[_rows(w) for w in _MIX_WIDTHS],
        out_shape=[jax.ShapeDtypeStruct((n, w), d) for w, d in zip(_MIX_WIDTHS, _MIX_DTYPES)],
        compiler_params=pltpu.CompilerParams(
            dimension_semantics=("arbitrary",), vmem_limit_bytes=V7X_VMEM_LIMIT_BYTES),
        name="mix_inproj",
    )(x1b, wmain, wdt)


def _hgrn_kernel(lbl_ref, ng_ref, suffix_ref, sumsel_ref, q_ref, f_ref, v_ref, g_ref, o_ref,
                 st_ref, f_sc, kk_sc, p_sc, a_sc, qa_sc, ke_sc, fall_sc):
    @pl.when(pl.program_id(1) == 0)
    def _():
        st_ref[...] = jnp.zeros_like(st_ref)

    logits = lbl_ref[...]
    ex = jnp.exp(logits - jnp.max(logits, axis=0, keepdims=True))
    lb = ex[0:1, :] / jnp.sum(ex, axis=0, keepdims=True)
    ng = ng_ref[...]

    row = lax.broadcasted_iota(jnp.int32, (CHUNK, CHUNK), 0)
    col = lax.broadcasted_iota(jnp.int32, (CHUNK, CHUNK), 1)
    diag_mask = (col // HALF == row // HALF) & (col <= row)
    pair_mask = col // SUB == row // SUB
    tloc = lax.broadcasted_iota(jnp.int32, (HALF, HG_DK), 0)
    zero_blk = jnp.zeros((SUB, HG_DK), BF16)

    def blk(x, i):
        return x[i * SUB:(i + 1) * SUB, :]

    def scores(c, buf):
        rows = pl.ds(pl.multiple_of(c * CHUNK, CHUNK), CHUNK)
        f = lb + (1.0 - lb) * _sigmoid(f_ref[0, rows, :])
        kk = 1.0 - f
        qq = _silu(q_ref[0, rows, :])
        f_sc[buf] = f
        kk_sc[buf] = kk
        d8 = jnp.exp2(_dot_sel(suffix_ref[...], jnp.log2(f)))
        kd8 = kk * d8
        f8 = [f[b * HALF:b * HALF + 1, :] * d8[b * HALF:b * HALF + 1, :] for b in range(2 * NSUB)]

        qp8 = []
        a_diag = []
        hsl = [slice(h * HG_DK, (h + 1) * HG_DK) for h in range(HG_HEADS)]
        for i in range(NSUB):
            q_lo = [qq[i * SUB:i * SUB + HALF, hs] for hs in hsl]
            q_hi = [qq[i * SUB + HALF:(i + 1) * SUB, hs] for hs in hsl]
            w_lo, w_hi = list(q_lo), list(q_hi)
            for s in range(HALF - 1, -1, -1):
                r_lo = i * SUB + s
                r_hi = r_lo + HALF
                keep = tloc >= s
                for h, hs in enumerate(hsl):
                    k_lo = jnp.broadcast_to(kk_sc[buf, r_lo:r_lo + 1, hs], (HALF, HG_DK))
                    k_hi = jnp.broadcast_to(kk_sc[buf, r_hi:r_hi + 1, hs], (HALF, HG_DK))
                    piece = jnp.concatenate([w_lo[h] * k_lo, w_hi[h] * k_hi], axis=0).astype(BF16)
                    p_sc[buf, i, h * SUB:(h + 1) * SUB, s * HG_DK:(s + 1) * HG_DK] = piece
                    w_lo[h] = w_lo[h] * jnp.broadcast_to(f_sc[buf, r_lo:r_lo + 1, hs], (HALF, HG_DK))
                    w_hi[h] = w_hi[h] * jnp.broadcast_to(f_sc[buf, r_hi:r_hi + 1, hs], (HALF, HG_DK))
                    if s > 0:
                        w_lo[h] = jnp.where(keep, w_lo[h], q_lo[h])
                        w_hi[h] = jnp.where(keep, w_hi[h], q_hi[h])
            qp8.append((jnp.concatenate(w_lo, axis=1), jnp.concatenate(w_hi, axis=1)))
            a_diag.append(_dot(p_sc[buf, i], sumsel_ref[...]))

        zero8 = jnp.zeros((HALF, HG_FDIM), F32)
        lhs16 = jnp.concatenate([x for i in range(NSUB) for x in (zero8, qp8[i][1])],
                                axis=0).astype(BF16)
        rhs16 = jnp.concatenate(
            [x for i in range(NSUB) for x in (kd8[i * SUB:i * SUB + HALF, :], zero8)],
            axis=0).astype(BF16)
        qp = [jnp.concatenate([qp8[i][0], qp8[i][1] * f8[2 * i]], axis=0) for i in range(NSUB)]
        kd = jnp.concatenate(
            [x for i in range(NSUB)
             for x in (kd8[i * SUB:i * SUB + HALF, :] * f8[2 * i + 1],
                       kd8[i * SUB + HALF:(i + 1) * SUB, :])], axis=0)
        fblk = [f8[2 * i] * f8[2 * i + 1] for i in range(NSUB)]

        f01 = fblk[0] * fblk[1]
        f12 = fblk[1] * fblk[2]
        f23 = fblk[2] * fblk[3]
        f012 = f01 * fblk[2]
        f123 = fblk[1] * f23
        fall_sc[buf, 0:1, :] = f01 * f23
        qa_sc[buf] = jnp.concatenate(
            [qp[0], qp[1] * fblk[0], qp[2] * f01, qp[3] * f012], axis=0).astype(BF16)
        ke_sc[buf] = jnp.concatenate(
            [blk(kd, 0) * f123, blk(kd, 1) * f23, blk(kd, 2) * fblk[3], blk(kd, 3)],
            axis=0).astype(BF16)
        qpb = [x.astype(BF16) for x in qp]
        kdb = [blk(kd, j).astype(BF16) for j in range(NSUB)]
        kd0_f1 = (blk(kd, 0) * fblk[1]).astype(BF16)
        kd0_f12 = (blk(kd, 0) * f12).astype(BF16)
        kd1_f2 = (blk(kd, 1) * fblk[2]).astype(BF16)
        for h in range(HG_HEADS):
            hs = slice(h * HG_DK, (h + 1) * HG_DK)
            z = zero_blk
            lhs = jnp.concatenate([
                jnp.concatenate([z, qpb[1][:, hs], z, z], axis=0),
                jnp.concatenate([z, z, qpb[2][:, hs], z], axis=0),
                jnp.concatenate([z, z, z, qpb[3][:, hs]], axis=0)], axis=1)
            rhs = jnp.concatenate([
                jnp.concatenate([kdb[0][:, hs], z, z, z], axis=0),
                jnp.concatenate([kd0_f1[:, hs], kdb[1][:, hs], z, z], axis=0),
                jnp.concatenate([kd0_f12[:, hs], kd1_f2[:, hs], kdb[2][:, hs], z], axis=0)],
                axis=1)
            a_diag_h = jnp.concatenate([a_diag[i][h * SUB:(h + 1) * SUB, :] for i in range(NSUB)],
                                       axis=0)
            a = (_dot_nt(lhs, rhs)
                 + jnp.where(pair_mask, _dot_nt(lhs16[:, hs], rhs16[:, hs]), 0.0)
                 + jnp.where(diag_mask, a_diag_h, 0.0))
            a_sc[buf, h] = a.astype(BF16)

    def apply(c, buf):
        rows = pl.ds(pl.multiple_of(c * CHUNK, CHUNK), CHUNK)
        f_all = fall_sc[buf, 0:1, :]
        for h in range(HG_HEADS):
            hs = slice(h * HG_DK, (h + 1) * HG_DK)
            vb = v_ref[0, rows, hs]
            st = st_ref[h]
            o = _dot(a_sc[buf, h], vb) + _dot_nt(qa_sc[buf, :, hs], st.astype(BF16))
            st_ref[h] = st * f_all[:, hs] + _dot_tn(vb, ke_sc[buf, :, hs])
            o = o * lax.rsqrt(jnp.mean(o * o, axis=-1, keepdims=True) + RMS_EPS) * ng
            o = o * _silu(g_ref[0, rows, hs])
            o_ref[0, rows, hs] = o.astype(o_ref.dtype)

    n_pairs = HGRN_TILE // CHUNK // 2
    scores(0, 0)

    def body(j, carry):
        c = 2 * j
        apply(c, 0)
        scores(c + 1, 1)
        scores(c + 2, 0)
        apply(c + 1, 1)
        return carry

    lax.fori_loop(0, n_pairs - 1, body, 0)
    last = 2 * n_pairs - 2
    apply(last, 0)
    scores(last + 1, 1)
    apply(last + 1, 1)


def _seq_spec(tile, width):
    return pl.BlockSpec((1, tile, width), lambda b, t: (b, t, 0))


def _hgrn_selectors():
    row = lax.broadcasted_iota(jnp.int32, (CHUNK, CHUNK), 0)
    col = lax.broadcasted_iota(jnp.int32, (CHUNK, CHUNK), 1)
    suffix = ((col // HALF == row // HALF) & (col > row)).astype(BF16)
    srow = lax.broadcasted_iota(jnp.int32, (HALF * HG_DK, CHUNK), 0)
    scol = lax.broadcasted_iota(jnp.int32, (HALF * HG_DK, CHUNK), 1)
    sumsel = (srow // HG_DK == scol % HALF).astype(BF16)
    return suffix, sumsel


def _hgrn_call(lb_logits, norm_g, hq, hf, hi, hg):
    bsz, seqlen, _ = hq.shape
    suffix, sumsel = _hgrn_selectors()
    return pl.pallas_call(
        _hgrn_kernel,
        grid=(bsz, seqlen // HGRN_TILE),
        in_specs=[_resident(lb_logits.shape), _resident(norm_g.shape),
                  _resident(suffix.shape), _resident(sumsel.shape),
                  _seq_spec(HGRN_TILE, HG_FDIM), _seq_spec(HGRN_TILE, HG_FDIM),
                  _seq_spec(HGRN_TILE, HG_WIDTH), _seq_spec(HGRN_TILE, HG_WIDTH)],
        out_specs=_seq_spec(HGRN_TILE, HG_WIDTH),
        out_shape=jax.ShapeDtypeStruct((bsz, seqlen, HG_WIDTH), BF16),
        scratch_shapes=[
            pltpu.VMEM((HG_HEADS, HG_DV, HG_DK), F32),
            pltpu.VMEM((2, CHUNK, HG_FDIM), F32),
            pltpu.VMEM((2, CHUNK, HG_FDIM), F32),
            pltpu.VMEM((2, NSUB, HG_HEADS * SUB, HALF * HG_DK), BF16),
            pltpu.VMEM((2, HG_HEADS, CHUNK, CHUNK), BF16),
            pltpu.VMEM((2, CHUNK, HG_FDIM), BF16),
            pltpu.VMEM((2, CHUNK, HG_FDIM), BF16),
            pltpu.VMEM((2, SUBLANES, HG_FDIM), F32),
        ],
        compiler_params=pltpu.CompilerParams(
            dimension_semantics=("arbitrary", "arbitrary"), vmem_limit_bytes=V7X_VMEM_LIMIT_BYTES),
        name="hgrn2_mixer",
    )(lb_logits, norm_g, suffix, sumsel, hq, hf, hi, hg)


def _ssd_kernel(cw_ref, cb_ref, dtb_ref, alog_ref, dexp_ref, ng_ref,
                expand_ref, tri_ref, upper_ref,
                z_ref, xbc_ref, dt_ref, o_ref, ht_ref, xbuf, xc_sc):
    @pl.when(pl.program_id(1) == 0)
    def _():
        ht_ref[...] = jnp.zeros_like(ht_ref)
        xbuf[...] = jnp.zeros_like(xbuf)

    dexp = dexp_ref[...]
    ng = ng_ref[...]
    rate = -jnp.exp(alog_ref[...]) * LOG2_E

    for r0 in range(0, SEQ_TILE, CONV_ROWS):
        if r0 == 0:
            win = jnp.concatenate([xbuf[...], xbc_ref[0, 0:CONV_ROWS, :]], axis=0)
        else:
            win = xbc_ref[0, r0 - SUBLANES:r0 + CONV_ROWS, :]
        conv = cb_ref[...] + cw_ref[SSD_CONV - 1:SSD_CONV, :] * win[SUBLANES:, :]
        for d in range(1, SSD_CONV):
            shifted = pltpu.roll(win, d, axis=0)[SUBLANES:, :]
            conv = conv + cw_ref[SSD_CONV - 1 - d:SSD_CONV - d, :] * shifted
        xc_sc[r0:r0 + CONV_ROWS, :] = _silu(conv)
    xbuf[...] = xbc_ref[0, SEQ_TILE - SUBLANES:SEQ_TILE, :]

    dt = _softplus(dt_ref[0] + dtb_ref[...])
    a = dt * rate
    expand = expand_ref[...]
    acum_h = _dot_sel(tri_ref[...], a)
    acum_all = _dot_x_sel(acum_h, expand)
    acum_t_all = _dot_x_sel(a.T, upper_ref[...])
    dtx_all = xc_sc[:, 0:SSD_WIDTH] * _dot_x_sel(dt, expand)

    row = lax.broadcasted_iota(jnp.int32, (SSD_Q, LANES), 0)
    lane = lax.broadcasted_iota(jnp.int32, (SSD_Q, LANES), 1)
    key = lane % SSD_HEADDIM
    first_head = lane < SSD_HEADDIM
    krow = lax.broadcasted_iota(jnp.int32, (SSD_HEADDIM, LANES), 1) < SSD_HEADDIM

    for c in range(SEQ_TILE // SSD_Q):
        rows = slice(c * SSD_Q, (c + 1) * SSD_Q)
        acum = acum_all[rows, :]
        acum_t = acum_t_all[:, rows]
        xs = xc_sc[rows, 0:SSD_WIDTH]
        dtx = dtx_all[rows, :]
        a_last = acum[SSD_Q - 1:SSD_Q, :]
        dtx_end = (dtx * jnp.exp2(a_last - acum)).astype(BF16)
        dtxb = dtx.astype(BF16)
        e_cum = jnp.exp2(acum)
        e_last = jnp.exp2(a_last)

        ys = []
        for g in range(SSD_GROUPS):
            gs = slice(g * SSD_GROUP_W, (g + 1) * SSD_GROUP_W)
            b0 = SSD_WIDTH + g * SSD_STATE
            c0 = SSD_WIDTH + SSD_GROUPS * SSD_STATE + g * SSD_STATE
            bm = xc_sc[rows, b0:b0 + SSD_STATE].astype(BF16)
            cm = xc_sc[rows, c0:c0 + SSD_STATE].astype(BF16)
            ht = ht_ref[g]
            y_grp = _dot(cm, ht.astype(BF16)) * e_cum[:, gs]
            y_pairs = [None] * (SSD_HPG // 2)
            for half in range(SSD_Q // SSD_HEADDIM):
                s0 = half * SSD_HEADDIM
                bm_half = bm[s0:s0 + SSD_HEADDIM, :]
                cb2 = _dot_nt(cm, jnp.concatenate([bm_half, bm_half], axis=0))
                causal2 = row >= key + s0
                for pr in range(SSD_HPG // 2):
                    h = g * SSD_HPG + 2 * pr
                    lanes = slice(h * SSD_HEADDIM, (h + 2) * SSD_HEADDIM)
                    keys_a = acum_t[h:h + 1, s0:s0 + SSD_HEADDIM]
                    keys_b = acum_t[h + 1:h + 2, s0:s0 + SSD_HEADDIM]
                    keys_t = jnp.where(first_head[0:1, :],
                                       jnp.concatenate([keys_a, keys_a], axis=1),
                                       jnp.concatenate([keys_b, keys_b], axis=1))
                    seg = acum[:, lanes] - keys_t
                    lmat = jnp.exp2(jnp.where(causal2, seg, NEG_BIG))
                    x_half = dtxb[s0:s0 + SSD_HEADDIM, lanes]
                    zero = jnp.zeros_like(x_half)
                    rhs = jnp.concatenate([jnp.where(krow, x_half, zero),
                                           jnp.where(krow, zero, x_half)], axis=0)
                    part = _dot((cb2 * lmat).astype(BF16), rhs)
                    y_pairs[pr] = part if y_pairs[pr] is None else y_pairs[pr] + part
            ys.append(jnp.concatenate(y_pairs, axis=1) + y_grp)
            ht_ref[g] = ht * e_last[:, gs] + _dot_tn(bm, dtx_end[:, gs])
        y = jnp.concatenate(ys, axis=1) + dexp * xs
        y = y * _silu(z_ref[0, rows, :])
        y = y * lax.rsqrt(jnp.mean(y * y, axis=-1, keepdims=True) + RMS_EPS) * ng
        o_ref[0, rows, :] = y.astype(o_ref.dtype)


def _ssd_selectors():
    hrow = lax.broadcasted_iota(jnp.int32, (DT_PAD, SSD_WIDTH), 0)
    hcol = lax.broadcasted_iota(jnp.int32, (DT_PAD, SSD_WIDTH), 1)
    expand = (hrow == hcol // SSD_HEADDIM).astype(BF16)
    row = lax.broadcasted_iota(jnp.int32, (SEQ_TILE, SEQ_TILE), 0)
    col = lax.broadcasted_iota(jnp.int32, (SEQ_TILE, SEQ_TILE), 1)
    same_chunk = row // SSD_Q == col // SSD_Q
    tri = (same_chunk & (col <= row)).astype(BF16)
    upper = (same_chunk & (row <= col)).astype(BF16)
    return expand, tri, upper


def _ssd_call(conv_w, conv_b, dt_bias, a_log, d_exp, norm_g, sz, sxbc, sdt):
    bsz, seqlen, _ = sz.shape
    params = (conv_w, conv_b, dt_bias, a_log, d_exp, norm_g) + _ssd_selectors()
    return pl.pallas_call(
        _ssd_kernel,
        grid=(bsz, seqlen // SEQ_TILE),
        in_specs=[_resident(p.shape) for p in params]
        + [_seq_spec(SEQ_TILE, SSD_WIDTH), _seq_spec(SEQ_TILE, SSD_CONV_CH),
           _seq_spec(SEQ_TILE, DT_PAD)],
        out_specs=_seq_spec(SEQ_TILE, SSD_WIDTH),
        out_shape=jax.ShapeDtypeStruct((bsz, seqlen, SSD_WIDTH), BF16),
        scratch_shapes=[
            pltpu.VMEM((SSD_GROUPS, SSD_STATE, SSD_GROUP_W), F32),
            pltpu.VMEM((SUBLANES, SSD_CONV_CH), F32),
            pltpu.VMEM((SEQ_TILE, SSD_CONV_CH), F32),
        ],
        compiler_params=pltpu.CompilerParams(
            dimension_semantics=("arbitrary", "arbitrary"), vmem_limit_bytes=V7X_VMEM_LIMIT_BYTES),
        name="ssd_mixer",
    )(*params, sz, sxbc, sdt)


def _post_kernel(oh_ref, os_ref, x1_ref, p_ref, wo_ref, g2_ref, b2_ref,
                 w2i_ref, w2o_ref, g3_ref, b3_ref, wg_ref, wp_ref, g4_ref, b4_ref, out_ref):
    x2 = []
    for rows in _SUBTILES:
        mix = (_dot(oh_ref[rows, :], wo_ref[0:HG_WIDTH, :])
               + _dot(os_ref[rows, :], wo_ref[HG_WIDTH:, :]))
        x2.append(_layer_norm(DEEPNORM_ALPHA * x1_ref[rows, :] + mix, g2_ref[...], b2_ref[...]))
    x3 = [_ffn_ln(x, w2i_ref, w2o_ref, g3_ref[...], b3_ref[...]) for x in x2]
    for rows, x in zip(_SUBTILES, x3):
        gate = _sigmoid(_dot(x.astype(BF16), wg_ref[...]))
        ple = gate * _dot(p_ref[rows, :].astype(BF16), wp_ref[...])
        out_ref[rows, :] = _layer_norm(DEEPNORM_ALPHA * x + ple, g4_ref[...], b4_ref[...])


def _post_call(oh, osd, x1, p2d, wo, g2, b2, w2i, w2o, g3, b3, wg, wp, g4, b4):
    n = x1.shape[0]
    weights = (wo, g2, b2, w2i, w2o, g3, b3, wg, wp, g4, b4)
    return pl.pallas_call(
        _post_kernel,
        grid=(n // TOKEN_TILE,),
        in_specs=[_rows(HG_WIDTH), _rows(SSD_WIDTH), _rows(D_MODEL), _rows(PLE_DIM)]
        + [_resident(w.shape) for w in weights],
        out_specs=_rows(D_MODEL),
        out_shape=jax.ShapeDtypeStruct((n, D_MODEL), F32),
        compiler_params=pltpu.CompilerParams(
            dimension_semantics=("arbitrary",), vmem_limit_bytes=V7X_VMEM_LIMIT_BYTES),
        name="post_outproj_ffn_ple",
    )(oh, osd, x1, p2d, *weights)


def _row(v):
    return v.reshape(1, -1).astype(F32)


def kernel(x, p, ffn1_w_in, ffn1_w_out, ln1_g, ln1_b, w_in_mix, hgrn_lb_logits, hgrn_norm_g,
           ssd_conv_w, ssd_conv_b, ssd_dt_bias, ssd_a_log, ssd_d, ssd_norm_g, w_out_mix,
           ln2_g, ln2_b, ffn2_w_in, ffn2_w_out, ln3_g, ln3_b, ple_w_proj, ple_w_gate,
           ln4_g, ln4_b):
    bsz, seqlen, _ = x.shape
    n = bsz * seqlen
    assert DEPTH == 1 and n % TOKEN_TILE == 0
    assert seqlen % SEQ_TILE == 0 and seqlen % HGRN_TILE == 0
    h = x.reshape(n, D_MODEL)
    for i in range(DEPTH):
        head_pad = DT_PAD - SSD_HEADS
        n_main = MIX_COLS_PADDED - DT_PAD
        wmain = w_in_mix[i][:, :n_main].astype(BF16)
        wdt = jnp.pad(w_in_mix[i][:, n_main:], ((0, 0), (0, head_pad))).astype(BF16)
        x1, x1b = _pre_call(h, ffn1_w_in[i].astype(BF16), ffn1_w_out[i].astype(BF16),
                            _row(ln1_g[i]), _row(ln1_b[i]))
        hq, hf, hi, hg, sz, sxbc, sdt = _proj_call(x1b, wmain, wdt)

        def seq(t):
            return t.reshape(bsz, seqlen, t.shape[-1])

        o_h = _hgrn_call(hgrn_lb_logits.astype(F32), _row(hgrn_norm_g[i]),
                         seq(hq), seq(hf), seq(hi), seq(hg))
        o_s = _ssd_call(
            ssd_conv_w[i].astype(F32), _row(ssd_conv_b[i]),
            jnp.pad(_row(ssd_dt_bias[i]), ((0, 0), (0, head_pad))),
            jnp.pad(_row(ssd_a_log[i]), ((0, 0), (0, head_pad))),
            _row(jnp.repeat(ssd_d[i], SSD_HEADDIM)), _row(ssd_norm_g[i]),
            seq(sz), seq(sxbc), seq(sdt))
        h = _post_call(
            o_h.reshape(n, HG_WIDTH), o_s.reshape(n, SSD_WIDTH), x1, p[i].reshape(n, PLE_DIM),
            w_out_mix[i].astype(BF16), _row(ln2_g[i]), _row(ln2_b[i]),
            ffn2_w_in[i].astype(BF16), ffn2_w_out[i].astype(BF16), _row(ln3_g[i]), _row(ln3_b[i]),
            ple_w_gate[i].astype(BF16), ple_w_proj[i].astype(BF16), _row(ln4_g[i]), _row(ln4_b[i]))
    return h.reshape(bsz, seqlen, D_MODEL)
```

```python
import jax
import jax.numpy as jnp
from jax import lax
from jax.experimental import pallas as pl
from jax.experimental.pallas import tpu as pltpu

F32 = jnp.float32
BF16 = jnp.bfloat16

DEPTH = 1
D_MODEL = 1024
D_FF = 2816
CHUNK = 64
PLE_DIM = 256
HG_HEADS = 4
HG_DK = 128
HG_DV = 128
HG_WIDTH = HG_HEADS * HG_DV
HG_FDIM = HG_HEADS * HG_DK
SSD_WIDTH = 512
SSD_HEADDIM = 64
SSD_HEADS = SSD_WIDTH // SSD_HEADDIM
SSD_GROUPS = 2
SSD_HPG = SSD_HEADS // SSD_GROUPS
SSD_STATE = 128
SSD_CONV = 4
SSD_CONV_CH = SSD_WIDTH + 2 * SSD_GROUPS * SSD_STATE
SSD_GROUP_W = SSD_HPG * SSD_HEADDIM
DEEPNORM_ALPHA = (2.0 * DEPTH) ** 0.25
LN_EPS = 1e-5
RMS_EPS = 1e-6
LOG2_E = 1.4426950408889634

LANES = 128
SUBLANES = 8
V7X_MXU_DIM = 256
V7X_VMEM_LIMIT_BYTES = 56 * 1024 * 1024

DT_PAD = LANES
MIX_COLS_PADDED = 2 * HG_FDIM + 2 * HG_WIDTH + SSD_WIDTH + SSD_CONV_CH + DT_PAD
TOKEN_TILE = 1024
SUBTILE = 256
_SUBTILES = tuple(slice(r, r + SUBTILE) for r in range(0, TOKEN_TILE, SUBTILE))
FF_SPLITS = (0, (D_FF // V7X_MXU_DIM + 1) // 2 * V7X_MXU_DIM, D_FF)
SEQ_TILE = 256
HGRN_TILE = 2048
SUB = 16
NSUB = CHUNK // SUB
HALF = SUB // 2
NEG_BIG = -1e30
SSD_Q = 128
CONV_ROWS = 64


def _sigmoid(x):
    return 1.0 / (1.0 + jnp.exp(-x))


def _silu(x):
    return x * _sigmoid(x)


def _softplus(x):
    return jnp.maximum(x, 0.0) + jnp.log1p(jnp.exp(-jnp.abs(x)))


def _dot(a, b):
    return jnp.dot(a, b, preferred_element_type=F32)


def _dot_nt(a, b):
    return lax.dot_general(a, b, (((1,), (1,)), ((), ())), preferred_element_type=F32)


def _dot_tn(a, b):
    return lax.dot_general(a, b, (((0,), (0,)), ((), ())), preferred_element_type=F32)


def _split3(x):
    hi = x.astype(BF16)
    r1 = x - hi.astype(F32)
    mid = r1.astype(BF16)
    lo = (r1 - mid.astype(F32)).astype(BF16)
    return hi, mid, lo


def _dot_sel(sel, x):
    hi, mid, lo = _split3(x)
    return _dot(sel, hi) + _dot(sel, mid) + _dot(sel, lo)


def _dot_x_sel(x, sel):
    hi, mid, lo = _split3(x)
    return _dot(hi, sel) + _dot(mid, sel) + _dot(lo, sel)


def _layer_norm(y, g, b):
    mu = jnp.mean(y, axis=-1, keepdims=True)
    d = y - mu
    var = jnp.mean(d * d, axis=-1, keepdims=True)
    return d * lax.rsqrt(var + LN_EPS) * g + b


def _ffn_ln(x, w_in_ref, w_out_ref, g, b):
    xb = x.astype(BF16)
    acc = None
    for c0, c1 in zip(FF_SPLITS[:-1], FF_SPLITS[1:]):
        gate = _dot(xb, w_in_ref[:, c0:c1])
        up = _dot(xb, w_in_ref[:, D_FF + c0:D_FF + c1])
        act = (_silu(gate) * up).astype(BF16)
        part = _dot(act, w_out_ref[c0:c1, :])
        acc = part if acc is None else acc + part
    return _layer_norm(DEEPNORM_ALPHA * x + 0.5 * acc, g, b)


def _pre_kernel(x_ref, w1i_ref, w1o_ref, g1_ref, b1_ref, x1_ref, x1b_ref):
    for rows in _SUBTILES:
        x1 = _ffn_ln(x_ref[rows, :], w1i_ref, w1o_ref, g1_ref[...], b1_ref[...])
        x1_ref[rows, :] = x1
        x1b_ref[rows, :] = x1.astype(BF16)


def _resident(shape):
    return pl.BlockSpec(shape, lambda *_: (0,) * len(shape), pipeline_mode=pl.Buffered(1))


def _rows(width):
    return pl.BlockSpec((TOKEN_TILE, width), lambda i: (i, 0))


def _pre_call(x2d, w1i, w1o, g1, b1):
    n = x2d.shape[0]
    return pl.pallas_call(
        _pre_kernel,
        grid=(n // TOKEN_TILE,),
        in_specs=[_rows(D_MODEL), _resident(w1i.shape), _resident(w1o.shape),
                  _resident(g1.shape), _resident(b1.shape)],
        out_specs=[_rows(D_MODEL), _rows(D_MODEL)],
        out_shape=[jax.ShapeDtypeStruct((n, D_MODEL), F32),
                   jax.ShapeDtypeStruct((n, D_MODEL), BF16)],
        compiler_params=pltpu.CompilerParams(
            dimension_semantics=("arbitrary",), vmem_limit_bytes=V7X_VMEM_LIMIT_BYTES),
        name="pre_ffn",
    )(x2d, w1i, w1o, g1, b1)


_MIX_WIDTHS = (HG_FDIM, HG_FDIM, HG_WIDTH, HG_WIDTH, SSD_WIDTH, SSD_CONV_CH, DT_PAD)
_MIX_DTYPES = (F32, F32, BF16, F32, F32, F32, F32)


def _proj_kernel(x1b_ref, wmain_ref, wdt_ref, *out_refs):
    c = 0
    for ref in out_refs[:-1]:
        w = ref.shape[-1]
        ref[...] = _dot_nt(x1b_ref[...], wmain_ref[c:c + w, :]).astype(ref.dtype)
        c += w
    out_refs[-1][...] = _dot_nt(x1b_ref[...], wdt_ref[...])


def _proj_call(x1b, wmain, wdt):
    n = x1b.shape[0]
    return pl.pallas_call(
        _proj_kernel,
        grid=(n // TOKEN_TILE,),
        in_specs=[_rows(D_MODEL), _resident(wmain.shape), _resident(wdt.shape)],
        out_specs=[_rows(w) for w in _MIX_WIDTHS],
        out_shape=[jax.ShapeDtypeStruct((n, w), d) for w, d in zip(_MIX_WIDTHS, _MIX_DTYPES)],
        compiler_params=pltpu.CompilerParams(
            dimension_semantics=("arbitrary",), vmem_limit_bytes=V7X_VMEM_LIMIT_BYTES),
        name="mix_inproj",
    )(x1b, wmain, wdt)


def _hgrn_kernel(lbl_ref, ng_ref, suffix_ref, sumsel_ref, q_ref, f_ref, v_ref, g_ref, o_ref,
                 st_ref, f_sc, kk_sc, p_sc, a_sc, qa_sc, ke_sc, fall_sc):
    @pl.when(pl.program_id(1) == 0)
    def _():
        st_ref[...] = jnp.zeros_like(st_ref)

    logits = lbl_ref[...]
    ex = jnp.exp(logits - jnp.max(logits, axis=0, keepdims=True))
    lb = ex[0:1, :] / jnp.sum(ex, axis=0, keepdims=True)
    ng = ng_ref[...]

    row = lax.broadcasted_iota(jnp.int32, (CHUNK, CHUNK), 0)
    col = lax.broadcasted_iota(jnp.int32, (CHUNK, CHUNK), 1)
    diag_mask = (col // HALF == row // HALF) & (col <= row)
    pair_mask = col // SUB == row // SUB
    tloc = lax.broadcasted_iota(jnp.int32, (HALF, HG_DK), 0)
    zero_blk = jnp.zeros((SUB, HG_DK), BF16)

    def blk(x, i):
        return x[i * SUB:(i + 1) * SUB, :]

    def scores(c, buf):
        rows = pl.ds(pl.multiple_of(c * CHUNK, CHUNK), CHUNK)
        f = lb + (1.0 - lb) * _sigmoid(f_ref[0, rows, :])
        kk = 1.0 - f
        qq = _silu(q_ref[0, rows, :])
        f_sc[buf] = f
        kk_sc[buf] = kk
        d8 = jnp.exp2(_dot_sel(suffix_ref[...], jnp.log2(f)))
        kd8 = kk * d8
        f8 = [f[b * HALF:b * HALF + 1, :] * d8[b * HALF:b * HALF + 1, :] for b in range(2 * NSUB)]

        qp8 = []
        a_diag = []
        hsl = [slice(h * HG_DK, (h + 1) * HG_DK) for h in range(HG_HEADS)]
        for i in range(NSUB):
            q_lo = [qq[i * SUB:i * SUB + HALF, hs] for hs in hsl]
            q_hi = [qq[i * SUB + HALF:(i + 1) * SUB, hs] for hs in hsl]
            w_lo, w_hi = list(q_lo), list(q_hi)
            for s in range(HALF - 1, -1, -1):
                r_lo = i * SUB + s
                r_hi = r_lo + HALF
                keep = tloc >= s
                for h, hs in enumerate(hsl):
                    k_lo = jnp.broadcast_to(kk_sc[buf, r_lo:r_lo + 1, hs], (HALF, HG_DK))
                    k_hi = jnp.broadcast_to(kk_sc[buf, r_hi:r_hi + 1, hs], (HALF, HG_DK))
                    piece = jnp.concatenate([w_lo[h] * k_lo, w_hi[h] * k_hi], axis=0).astype(BF16)
                    p_sc[buf, i, h * SUB:(h + 1) * SUB, s * HG_DK:(s + 1) * HG_DK] = piece
                    w_lo[h] = w_lo[h] * jnp.broadcast_to(f_sc[buf, r_lo:r_lo + 1, hs], (HALF, HG_DK))
                    w_hi[h] = w_hi[h] * jnp.broadcast_to(f_sc[buf, r_hi:r_hi + 1, hs], (HALF, HG_DK))
                    if s > 0:
                        w_lo[h] = jnp.where(keep, w_lo[h], q_lo[h])
                        w_hi[h] = jnp.where(keep, w_hi[h], q_hi[h])
            qp8.append((jnp.concatenate(w_lo, axis=1), jnp.concatenate(w_hi, axis=1)))
            a_diag.append(_dot(p_sc[buf, i], sumsel_ref[...]))

        zero8 = jnp.zeros((HALF, HG_FDIM), F32)
        lhs16 = jnp.concatenate([x for i in range(NSUB) for x in (zero8, qp8[i][1])],
                                axis=0).astype(BF16)
        rhs16 = jnp.concatenate(
            [x for i in range(NSUB) for x in (kd8[i * SUB:i * SUB + HALF, :], zero8)],
            axis=0).astype(BF16)
        qp = [jnp.concatenate([qp8[i][0], qp8[i][1] * f8[2 * i]], axis=0) for i in range(NSUB)]
        kd = jnp.concatenate(
            [x for i in range(NSUB)
             for x in (kd8[i * SUB:i * SUB + HALF, :] * f8[2 * i + 1],
                       kd8[i * SUB + HALF:(i + 1) * SUB, :])], axis=0)
        fblk = [f8[2 * i] * f8[2 * i + 1] for i in range(NSUB)]

        f01 = fblk[0] * fblk[1]
        f12 = fblk[1] * fblk[2]
        f23 = fblk[2] * fblk[3]
        f012 = f01 * fblk[2]
        f123 = fblk[1] * f23
        fall_sc[buf, 0:1, :] = f01 * f23
        qa_sc[buf] = jnp.concatenate(
            [qp[0], qp[1] * fblk[0], qp[2] * f01, qp[3] * f012], axis=0).astype(BF16)
        ke_sc[buf] = jnp.concatenate(
            [blk(kd, 0) * f123, blk(kd, 1) * f23, blk(kd, 2) * fblk[3], blk(kd, 3)],
            axis=0).astype(BF16)
        qpb = [x.astype(BF16) for x in qp]
        kdb = [blk(kd, j).astype(BF16) for j in range(NSUB)]
        kd0_f1 = (blk(kd, 0) * fblk[1]).astype(BF16)
        kd0_f12 = (blk(kd, 0) * f12).astype(BF16)
        kd1_f2 = (blk(kd, 1) * fblk[2]).astype(BF16)
        for h in range(HG_HEADS):
            hs = slice(h * HG_DK, (h + 1) * HG_DK)
            z = zero_blk
            lhs = jnp.concatenate([
                jnp.concatenate([z, qpb[1][:, hs], z, z], axis=0),
                jnp.concatenate([z, z, qpb[2][:, hs], z], axis=0),
                jnp.concatenate([z, z, z, qpb[3][:, hs]], axis=0)], axis=1)
            rhs = jnp.concatenate([
                jnp.concatenate([kdb[0][:, hs], z, z, z], axis=0),
                jnp.concatenate([kd0_f1[:, hs], kdb[1][:, hs], z, z], axis=0),
                jnp.concatenate([kd0_f12[:, hs], kd1_f2[:, hs], kdb[2][:, hs], z], axis=0)],
                axis=1)
            a_diag_h = jnp.concatenate([a_diag[i][h * SUB:(h + 1) * SUB, :] for i in range(NSUB)],
                                       axis=0)
            a = (_dot_nt(lhs, rhs)
                 + jnp.where(pair_mask, _dot_nt(lhs16[:, hs], rhs16[:, hs]), 0.0)
                 + jnp.where(diag_mask, a_diag_h, 0.0))
            a_sc[buf, h] = a.astype(BF16)

    def apply(c, buf):
        rows = pl.ds(pl.multiple_of(c * CHUNK, CHUNK), CHUNK)
        f_all = fall_sc[buf, 0:1, :]
        for h in range(HG_HEADS):
            hs = slice(h * HG_DK, (h + 1) * HG_DK)
            vb = v_ref[0, rows, hs]
            st = st_ref[h]
            o = _dot(a_sc[buf, h], vb) + _dot_nt(qa_sc[buf, :, hs], st.astype(BF16))
            st_ref[h] = st * f_all[:, hs] + _dot_tn(vb, ke_sc[buf, :, hs])
            o = o * lax.rsqrt(jnp.mean(o * o, axis=-1, keepdims=True) + RMS_EPS) * ng
            o = o * _silu(g_ref[0, rows, hs])
            o_ref[0, rows, hs] = o.astype(o_ref.dtype)

    n_pairs = HGRN_TILE // CHUNK // 2
    scores(0, 0)

    def body(j, carry):
        c = 2 * j
        apply(c, 0)
        scores(c + 1, 1)
        scores(c + 2, 0)
        apply(c + 1, 1)
        return carry

    lax.fori_loop(0, n_pairs - 1, body, 0)
    last = 2 * n_pairs - 2
    apply(last, 0)
    scores(last + 1, 1)
    apply(last + 1, 1)


def _seq_spec(tile, width):
    return pl.BlockSpec((1, tile, width), lambda b, t: (b, t, 0))


def _hgrn_selectors():
    row = lax.broadcasted_iota(jnp.int32, (CHUNK, CHUNK), 0)
    col = lax.broadcasted_iota(jnp.int32, (CHUNK, CHUNK), 1)
    suffix = ((col // HALF == row // HALF) & (col > row)).astype(BF16)
    srow = lax.broadcasted_iota(jnp.int32, (HALF * HG_DK, CHUNK), 0)
    scol = lax.broadcasted_iota(jnp.int32, (HALF * HG_DK, CHUNK), 1)
    sumsel = (srow // HG_DK == scol % HALF).astype(BF16)
    return suffix, sumsel


def _hgrn_call(lb_logits, norm_g, hq, hf, hi, hg):
    bsz, seqlen, _ = hq.shape
    suffix, sumsel = _hgrn_selectors()
    return pl.pallas_call(
        _hgrn_kernel,
        grid=(bsz, seqlen // HGRN_TILE),
        in_specs=[_resident(lb_logits.shape), _resident(norm_g.shape),
                  _resident(suffix.shape), _resident(sumsel.shape),
                  _seq_spec(HGRN_TILE, HG_FDIM), _seq_spec(HGRN_TILE, HG_FDIM),
                  _seq_spec(HGRN_TILE, HG_WIDTH), _seq_spec(HGRN_TILE, HG_WIDTH)],
        out_specs=_seq_spec(HGRN_TILE, HG_WIDTH),
        out_shape=jax.ShapeDtypeStruct((bsz, seqlen, HG_WIDTH), BF16),
        scratch_shapes=[
            pltpu.VMEM((HG_HEADS, HG_DV, HG_DK), F32),
            pltpu.VMEM((2, CHUNK, HG_FDIM), F32),
            pltpu.VMEM((2, CHUNK, HG_FDIM), F32),
            pltpu.VMEM((2, NSUB, HG_HEADS * SUB, HALF * HG_DK), BF16),
            pltpu.VMEM((2, HG_HEADS, CHUNK, CHUNK), BF16),
            pltpu.VMEM((2, CHUNK, HG_FDIM), BF16),
            pltpu.VMEM((2, CHUNK, HG_FDIM), BF16),
            pltpu.VMEM((2, SUBLANES, HG_FDIM), F32),
        ],
        compiler_params=pltpu.CompilerParams(
            dimension_semantics=("arbitrary", "arbitrary"), vmem_limit_bytes=V7X_VMEM_LIMIT_BYTES),
        name="hgrn2_mixer",
    )(lb_logits, norm_g, suffix, sumsel, hq, hf, hi, hg)


def _ssd_kernel(cw_ref, cb_ref, dtb_ref, alog_ref, dexp_ref, ng_ref,
                expand_ref, tri_ref, upper_ref,
                z_ref, xbc_ref, dt_ref, o_ref, ht_ref, xbuf, xc_sc):
    @pl.when(pl.program_id(1) == 0)
    def _():
        ht_ref[...] = jnp.zeros_like(ht_ref)
        xbuf[...] = jnp.zeros_like(xbuf)

    dexp = dexp_ref[...]
    ng = ng_ref[...]
    rate = -jnp.exp(alog_ref[...]) * LOG2_E

    for r0 in range(0, SEQ_TILE, CONV_ROWS):
        if r0 == 0:
            win = jnp.concatenate([xbuf[...], xbc_ref[0, 0:CONV_ROWS, :]], axis=0)
        else:
            win = xbc_ref[0, r0 - SUBLANES:r0 + CONV_ROWS, :]
        conv = cb_ref[...] + cw_ref[SSD_CONV - 1:SSD_CONV, :] * win[SUBLANES:, :]
        for d in range(1, SSD_CONV):
            shifted = pltpu.roll(win, d, axis=0)[SUBLANES:, :]
            conv = conv + cw_ref[SSD_CONV - 1 - d:SSD_CONV - d, :] * shifted
        xc_sc[r0:r0 + CONV_ROWS, :] = _silu(conv)
    xbuf[...] = xbc_ref[0, SEQ_TILE - SUBLANES:SEQ_TILE, :]

    dt = _softplus(dt_ref[0] + dtb_ref[...])
    a = dt * rate
    expand = expand_ref[...]
    acum_h = _dot_sel(tri_ref[...], a)
    acum_all = _dot_x_sel(acum_h, expand)
    acum_t_all = _dot_x_sel(a.T, upper_ref[...])
    dtx_all = xc_sc[:, 0:SSD_WIDTH] * _dot_x_sel(dt, expand)

    row = lax.broadcasted_iota(jnp.int32, (SSD_Q, LANES), 0)
    lane = lax.broadcasted_iota(jnp.int32, (SSD_Q, LANES), 1)
    key = lane % SSD_HEADDIM
    first_head = lane < SSD_HEADDIM
    krow = lax.broadcasted_iota(jnp.int32, (SSD_HEADDIM, LANES), 1) < SSD_HEADDIM

    for c in range(SEQ_TILE // SSD_Q):
        rows = slice(c * SSD_Q, (c + 1) * SSD_Q)
        acum = acum_all[rows, :]
        acum_t = acum_t_all[:, rows]
        xs = xc_sc[rows, 0:SSD_WIDTH]
        dtx = dtx_all[rows, :]
        a_last = acum[SSD_Q - 1:SSD_Q, :]
        dtx_end = (dtx * jnp.exp2(a_last - acum)).astype(BF16)
        dtxb = dtx.astype(BF16)
        e_cum = jnp.exp2(acum)
        e_last = jnp.exp2(a_last)

        ys = []
        for g in range(SSD_GROUPS):
            gs = slice(g * SSD_GROUP_W, (g + 1) * SSD_GROUP_W)
            b0 = SSD_WIDTH + g * SSD_STATE
            c0 = SSD_WIDTH + SSD_GROUPS * SSD_STATE + g * SSD_STATE
            bm = xc_sc[rows, b0:b0 + SSD_STATE].astype(BF16)
            cm = xc_sc[rows, c0:c0 + SSD_STATE].astype(BF16)
            ht = ht_ref[g]
            y_grp = _dot(cm, ht.astype(BF16)) * e_cum[:, gs]
            y_pairs = [None] * (SSD_HPG // 2)
            for half in range(SSD_Q // SSD_HEADDIM):
                s0 = half * SSD_HEADDIM
                bm_half = bm[s0:s0 + SSD_HEADDIM, :]
                cb2 = _dot_nt(cm, jnp.concatenate([bm_half, bm_half], axis=0))
                causal2 = row >= key + s0
                for pr in range(SSD_HPG // 2):
                    h = g * SSD_HPG + 2 * pr
                    lanes = slice(h * SSD_HEADDIM, (h + 2) * SSD_HEADDIM)
                    keys_a = acum_t[h:h + 1, s0:s0 + SSD_HEADDIM]
                    keys_b = acum_t[h + 1:h + 2, s0:s0 + SSD_HEADDIM]
                    keys_t = jnp.where(first_head[0:1, :],
                                       jnp.concatenate([keys_a, keys_a], axis=1),
                                       jnp.concatenate([keys_b, keys_b], axis=1))
                    seg = acum[:, lanes] - keys_t
                    lmat = jnp.exp2(jnp.where(causal2, seg, NEG_BIG))
                    x_half = dtxb[s0:s0 + SSD_HEADDIM, lanes]
                    zero = jnp.zeros_like(x_half)
                    rhs = jnp.concatenate([jnp.where(krow, x_half, zero),
                                           jnp.where(krow, zero, x_half)], axis=0)
                    part = _dot((cb2 * lmat).astype(BF16), rhs)
                    y_pairs[pr] = part if y_pairs[pr] is None else y_pairs[pr] + part
            ys.append(jnp.concatenate(y_pairs, axis=1) + y_grp)
            ht_ref[g] = ht * e_last[:, gs] + _dot_tn(bm, dtx_end[:, gs])
        y = jnp.concatenate(ys, axis=1) + dexp * xs
        y = y * _silu(z_ref[0, rows, :])
        y = y * lax.rsqrt(jnp.mean(y * y, axis=-1, keepdims=True) + RMS_EPS) * ng
        o_ref[0, rows, :] = y.astype(o_ref.dtype)


def _ssd_selectors():
    hrow = lax.broadcasted_iota(jnp.int32, (DT_PAD, SSD_WIDTH), 0)
    hcol = lax.broadcasted_iota(jnp.int32, (DT_PAD, SSD_WIDTH), 1)
    expand = (hrow == hcol // SSD_HEADDIM).astype(BF16)
    row = lax.broadcasted_iota(jnp.int32, (SEQ_TILE, SEQ_TILE), 0)
    col = lax.broadcasted_iota(jnp.int32, (SEQ_TILE, SEQ_TILE), 1)
    same_chunk = row // SSD_Q == col // SSD_Q
    tri = (same_chunk & (col <= row)).astype(BF16)
    upper = (same_chunk & (row <= col)).astype(BF16)
    return expand, tri, upper


def _ssd_call(conv_w, conv_b, dt_bias, a_log, d_exp, norm_g, sz, sxbc, sdt):
    bsz, seqlen, _ = sz.shape
    params = (conv_w, conv_b, dt_bias, a_log, d_exp, norm_g) + _ssd_selectors()
    return pl.pallas_call(
        _ssd_kernel,
        grid=(bsz, seqlen // SEQ_TILE),
        in_specs=[_resident(p.shape) for p in params]
        + [_seq_spec(SEQ_TILE, SSD_WIDTH), _seq_spec(SEQ_TILE, SSD_CONV_CH),
           _seq_spec(SEQ_TILE, DT_PAD)],
        out_specs=_seq_spec(SEQ_TILE, SSD_WIDTH),
        out_shape=jax.ShapeDtypeStruct((bsz, seqlen, SSD_WIDTH), BF16),
        scratch_shapes=[
            pltpu.VMEM((SSD_GROUPS, SSD_STATE, SSD_GROUP_W), F32),
            pltpu.VMEM((SUBLANES, SSD_CONV_CH), F32),
            pltpu.VMEM((SEQ_TILE, SSD_CONV_CH), F32),
        ],
        compiler_params=pltpu.CompilerParams(
            dimension_semantics=("arbitrary", "arbitrary"), vmem_limit_bytes=V7X_VMEM_LIMIT_BYTES),
        name="ssd_mixer",
    )(*params, sz, sxbc, sdt)


def _post_kernel(oh_ref, os_ref, x1_ref, p_ref, wo_ref, g2_ref, b2_ref,
                 w2i_ref, w2o_ref, g3_ref, b3_ref, wg_ref, wp_ref, g4_ref, b4_ref, out_ref):
    x2 = []
    for rows in _SUBTILES:
        mix = (_dot(oh_ref[rows, :], wo_ref[0:HG_WIDTH, :])
               + _dot(os_ref[rows, :], wo_ref[HG_WIDTH:, :]))
        x2.append(_layer_norm(DEEPNORM_ALPHA * x1_ref[rows, :] + mix, g2_ref[...], b2_ref[...]))
    x3 = [_ffn_ln(x, w2i_ref, w2o_ref, g3_ref[...], b3_ref[...]) for x in x2]
    for rows, x in zip(_SUBTILES, x3):
        gate = _sigmoid(_dot(x.astype(BF16), wg_ref[...]))
        ple = gate * _dot(p_ref[rows, :].astype(BF16), wp_ref[...])
        out_ref[rows, :] = _layer_norm(DEEPNORM_ALPHA * x + ple, g4_ref[...], b4_ref[...])


def _post_call(oh, osd, x1, p2d, wo, g2, b2, w2i, w2o, g3, b3, wg, wp, g4, b4):
    n = x1.shape[0]
    weights = (wo, g2, b2, w2i, w2o, g3, b3, wg, wp, g4, b4)
    return pl.pallas_call(
        _post_kernel,
        grid=(n // TOKEN_TILE,),
        in_specs=[_rows(HG_WIDTH), _rows(SSD_WIDTH), _rows(D_MODEL), _rows(PLE_DIM)]
        + [_resident(w.shape) for w in weights],
        out_specs=_rows(D_MODEL),
        out_shape=jax.ShapeDtypeStruct((n, D_MODEL), F32),
        compiler_params=pltpu.CompilerParams(
            dimension_semantics=("arbitrary",), vmem_limit_bytes=V7X_VMEM_LIMIT_BYTES),
        name="post_outproj_ffn_ple",
    )(oh, osd, x1, p2d, *weights)


def _row(v):
    return v.reshape(1, -1).astype(F32)


def kernel(x, p, ffn1_w_in, ffn1_w_out, ln1_g, ln1_b, w_in_mix, hgrn_lb_logits, hgrn_norm_g,
           ssd_conv_w, ssd_conv_b, ssd_dt_bias, ssd_a_log, ssd_d, ssd_norm_g, w_out_mix,
           ln2_g, ln2_b, ffn2_w_in, ffn2_w_out, ln3_g, ln3_b, ple_w_proj, ple_w_gate,
           ln4_g, ln4_b):
    bsz, seqlen, _ = x.shape
    n = bsz * seqlen
    assert DEPTH == 1 and n % TOKEN_TILE == 0
    assert seqlen % SEQ_TILE == 0 and seqlen % HGRN_TILE == 0
    h = x.reshape(n, D_MODEL)
    for i in range(DEPTH):
        head_pad = DT_PAD - SSD_HEADS
        n_main = MIX_COLS_PADDED - DT_PAD
        w_t = w_in_mix[i].T
        wmain = w_t[:n_main, :].astype(BF16)
        wdt = jnp.pad(w_t[n_main:, :], ((0, head_pad), (0, 0))).astype(BF16)
        x1, x1b = _pre_call(h, ffn1_w_in[i].astype(BF16), ffn1_w_out[i].astype(BF16),
                            _row(ln1_g[i]), _row(ln1_b[i]))
        hq, hf, hi, hg, sz, sxbc, sdt = _proj_call(x1b, wmain, wdt)

        def seq(t):
            return t.reshape(bsz, seqlen, t.shape[-1])

        o_h = _hgrn_call(hgrn_lb_logits.astype(F32), _row(hgrn_norm_g[i]),
                         seq(hq), seq(hf), seq(hi), seq(hg))
        o_s = _ssd_call(
            ssd_conv_w[i].astype(F32), _row(ssd_conv_b[i]),
            jnp.pad(_row(ssd_dt_bias[i]), ((0, 0), (0, head_pad))),
            jnp.pad(_row(ssd_a_log[i]), ((0, 0), (0, head_pad))),
            _row(jnp.repeat(ssd_d[i], SSD_HEADDIM)), _row(ssd_norm_g[i]),
            seq(sz), seq(sxbc), seq(sdt))
        h = _post_call(
            o_h.reshape(n, HG_WIDTH), o_s.reshape(n, SSD_WIDTH), x1, p[i].reshape(n, PLE_DIM),
            w_out_mix[i].astype(BF16), _row(ln2_g[i]), _row(ln2_b[i]),
            ffn2_w_in[i].astype(BF16), ffn2_w_out[i].astype(BF16), _row(ln3_g[i]), _row(ln3_b[i]),
            ple_w_gate[i].astype(BF16), ple_w_proj[i].astype(BF16), _row(ln4_g[i]), _row(ln4_b[i]))
    return h.reshape(bsz, seqlen, D_MODEL)
```

```python
import jax
import jax.numpy as jnp
from jax import lax
from jax.experimental import pallas as pl
from jax.experimental.pallas import tpu as pltpu

F32 = jnp.float32
BF16 = jnp.bfloat16

DEPTH = 1
D_MODEL = 1024
D_FF = 2816
CHUNK = 64
PLE_DIM = 256
HG_HEADS = 4
HG_DK = 128
HG_DV = 128
HG_WIDTH = HG_HEADS * HG_DV
HG_FDIM = HG_HEADS * HG_DK
SSD_WIDTH = 512
SSD_HEADDIM = 64
SSD_HEADS = SSD_WIDTH // SSD_HEADDIM
SSD_GROUPS = 2
SSD_HPG = SSD_HEADS // SSD_GROUPS
SSD_STATE = 128
SSD_CONV = 4
SSD_CONV_CH = SSD_WIDTH + 2 * SSD_GROUPS * SSD_STATE
SSD_GROUP_W = SSD_HPG * SSD_HEADDIM
DEEPNORM_ALPHA = (2.0 * DEPTH) ** 0.25
LN_EPS = 1e-5
RMS_EPS = 1e-6
LOG2_E = 1.4426950408889634

LANES = 128
SUBLANES = 8
V7X_MXU_DIM = 256
V7X_VMEM_LIMIT_BYTES = 56 * 1024 * 1024

DT_PAD = LANES
MIX_COLS_PADDED = 2 * HG_FDIM + 2 * HG_WIDTH + SSD_WIDTH + SSD_CONV_CH + DT_PAD
TOKEN_TILE = 1024
SUBTILE = 256
_SUBTILES = tuple(slice(r, r + SUBTILE) for r in range(0, TOKEN_TILE, SUBTILE))
FF_SPLITS = (0, (D_FF // V7X_MXU_DIM + 1) // 2 * V7X_MXU_DIM, D_FF)
SEQ_TILE = 256
HGRN_TILE = 2048
SUB = 16
NSUB = CHUNK // SUB
HALF = SUB // 2
NEG_BIG = -1e30
SSD_Q = 128
CONV_ROWS = 64


def _sigmoid(x):
    return 1.0 / (1.0 + jnp.exp(-x))


def _silu(x):
    return x * _sigmoid(x)


def _softplus(x):
    return jnp.maximum(x, 0.0) + jnp.log1p(jnp.exp(-jnp.abs(x)))


def _dot(a, b):
    return jnp.dot(a, b, preferred_element_type=F32)


def _dot_nt(a, b):
    return lax.dot_general(a, b, (((1,), (1,)), ((), ())), preferred_element_type=F32)


def _dot_tn(a, b):
    return lax.dot_general(a, b, (((0,), (0,)), ((), ())), preferred_element_type=F32)


def _split3(x):
    hi = x.astype(BF16)
    r1 = x - hi.astype(F32)
    mid = r1.astype(BF16)
    lo = (r1 - mid.astype(F32)).astype(BF16)
    return hi, mid, lo


def _dot_sel(sel, x):
    hi, mid, lo = _split3(x)
    return _dot(sel, hi) + _dot(sel, mid) + _dot(sel, lo)


def _dot_x_sel(x, sel):
    hi, mid, lo = _split3(x)
    return _dot(hi, sel) + _dot(mid, sel) + _dot(lo, sel)


def _layer_norm(y, g, b):
    mu = jnp.mean(y, axis=-1, keepdims=True)
    d = y - mu
    var = jnp.mean(d * d, axis=-1, keepdims=True)
    return d * lax.rsqrt(var + LN_EPS) * g + b


def _ffn_ln(x, w_in_ref, w_out_ref, g, b):
    xb = x.astype(BF16)
    acc = None
    for c0, c1 in zip(FF_SPLITS[:-1], FF_SPLITS[1:]):
        gate = _dot(xb, w_in_ref[:, c0:c1])
        up = _dot(xb, w_in_ref[:, D_FF + c0:D_FF + c1])
        act = (_silu(gate) * up).astype(BF16)
        part = _dot(act, w_out_ref[c0:c1, :])
        acc = part if acc is None else acc + part
    return _layer_norm(DEEPNORM_ALPHA * x + 0.5 * acc, g, b)


def _pre_kernel(x_ref, w1i_ref, w1o_ref, g1_ref, b1_ref, x1_ref, x1b_ref):
    for rows in _SUBTILES:
        x1 = _ffn_ln(x_ref[rows, :], w1i_ref, w1o_ref, g1_ref[...], b1_ref[...])
        x1_ref[rows, :] = x1
        x1b_ref[rows, :] = x1.astype(BF16)


def _resident(shape):
    return pl.BlockSpec(shape, lambda *_: (0,) * len(shape), pipeline_mode=pl.Buffered(1))


def _rows(width):
    return pl.BlockSpec((TOKEN_TILE, width), lambda i: (i, 0))


def _pre_call(x2d, w1i, w1o, g1, b1):
    n = x2d.shape[0]
    return pl.pallas_call(
        _pre_kernel,
        grid=(n // TOKEN_TILE,),
        in_specs=[_rows(D_MODEL), _resident(w1i.shape), _resident(w1o.shape),
                  _resident(g1.shape), _resident(b1.shape)],
        out_specs=[_rows(D_MODEL), _rows(D_MODEL)],
        out_shape=[jax.ShapeDtypeStruct((n, D_MODEL), F32),
                   jax.ShapeDtypeStruct((n, D_MODEL), BF16)],
        compiler_params=pltpu.CompilerParams(
            dimension_semantics=("arbitrary",), vmem_limit_bytes=V7X_VMEM_LIMIT_BYTES),
        name="pre_ffn",
    )(x2d, w1i, w1o, g1, b1)


_MIX_WIDTHS = (HG_FDIM, HG_FDIM, HG_WIDTH, HG_WIDTH, SSD_WIDTH, SSD_CONV_CH, DT_PAD)
_MIX_DTYPES = (F32, F32, BF16, F32, F32, F32, F32)


def _proj_kernel(x1b_ref, wmix_ref, *out_refs):
    c = 0
    for ref in out_refs:
        w = ref.shape[-1]
        ref[...] = _dot(x1b_ref[...], wmix_ref[:, c:c + w]).astype(ref.dtype)
        c += w


def _proj_call(x1b, wmix):
    n = x1b.shape[0]
    return pl.pallas_call(
        _proj_kernel,
        grid=(n // TOKEN_TILE,),
        in_specs=[_rows(D_MODEL), _resident(wmix.shape)],
        out_specs=[_rows(w) for w in _MIX_WIDTHS],
        out_shape=[jax.ShapeDtypeStruct((n, w), d) for w, d in zip(_MIX_WIDTHS, _MIX_DTYPES)],
        compiler_params=pltpu.CompilerParams(
            dimension_semantics=("arbitrary",), vmem_limit_bytes=V7X_VMEM_LIMIT_BYTES),
        name="mix_inproj",
    )(x1b, wmix)


def _hgrn_kernel(lbl_ref, ng_ref, suffix_ref, sumsel_ref, q_ref, f_ref, v_ref, g_ref, o_ref,
                 st_ref, p_sc, a_sc, qa_sc, ke_sc, fall_sc):
    @pl.when(pl.program_id(1) == 0)
    def _():
        st_ref[...] = jnp.zeros_like(st_ref)

    logits = lbl_ref[...]
    ex = jnp.exp(logits - jnp.max(logits, axis=0, keepdims=True))
    lb = ex[0:1, :] / jnp.sum(ex, axis=0, keepdims=True)
    ng = ng_ref[...]

    row = lax.broadcasted_iota(jnp.int32, (CHUNK, CHUNK), 0)
    col = lax.broadcasted_iota(jnp.int32, (CHUNK, CHUNK), 1)
    diag_mask = (col // HALF == row // HALF) & (col <= row)
    pair_mask = col // SUB == row // SUB
    tloc = lax.broadcasted_iota(jnp.int32, (HALF, HG_DK), 0)
    zero_blk = jnp.zeros((SUB, HG_DK), BF16)

    def blk(x, i):
        return x[i * SUB:(i + 1) * SUB, :]

    def scores(c, buf):
        rows = pl.ds(pl.multiple_of(c * CHUNK, CHUNK), CHUNK)
        f = lb + (1.0 - lb) * _sigmoid(f_ref[0, rows, :])
        kk = 1.0 - f
        qq = _silu(q_ref[0, rows, :])
        d8 = jnp.exp2(_dot_sel(suffix_ref[...], jnp.log2(f)))
        kd8 = kk * d8
        f8 = [f[b * HALF:b * HALF + 1, :] * d8[b * HALF:b * HALF + 1, :] for b in range(2 * NSUB)]

        qp8 = []
        a_diag = []
        hsl = [slice(h * HG_DK, (h + 1) * HG_DK) for h in range(HG_HEADS)]
        for i in range(NSUB):
            q_lo = [qq[i * SUB:i * SUB + HALF, hs] for hs in hsl]
            q_hi = [qq[i * SUB + HALF:(i + 1) * SUB, hs] for hs in hsl]
            w_lo, w_hi = list(q_lo), list(q_hi)
            for s in range(HALF - 1, -1, -1):
                r_lo = i * SUB + s
                r_hi = r_lo + HALF
                keep = tloc >= s
                for h, hs in enumerate(hsl):
                    k_lo = jnp.broadcast_to(kk[r_lo:r_lo + 1, hs], (HALF, HG_DK))
                    k_hi = jnp.broadcast_to(kk[r_hi:r_hi + 1, hs], (HALF, HG_DK))
                    piece = jnp.concatenate([w_lo[h] * k_lo, w_hi[h] * k_hi], axis=0).astype(BF16)
                    p_sc[buf, i, h * SUB:(h + 1) * SUB, s * HG_DK:(s + 1) * HG_DK] = piece
                    w_lo[h] = w_lo[h] * jnp.broadcast_to(f[r_lo:r_lo + 1, hs], (HALF, HG_DK))
                    w_hi[h] = w_hi[h] * jnp.broadcast_to(f[r_hi:r_hi + 1, hs], (HALF, HG_DK))
                    if s > 0:
                        w_lo[h] = jnp.where(keep, w_lo[h], q_lo[h])
                        w_hi[h] = jnp.where(keep, w_hi[h], q_hi[h])
            qp8.append((jnp.concatenate(w_lo, axis=1), jnp.concatenate(w_hi, axis=1)))
            a_diag.append(_dot(p_sc[buf, i], sumsel_ref[...]))

        zero8 = jnp.zeros((HALF, HG_FDIM), F32)
        lhs16 = jnp.concatenate([x for i in range(NSUB) for x in (zero8, qp8[i][1])],
                                axis=0).astype(BF16)
        rhs16 = jnp.concatenate(
            [x for i in range(NSUB) for x in (kd8[i * SUB:i * SUB + HALF, :], zero8)],
            axis=0).astype(BF16)
        qp = [jnp.concatenate([qp8[i][0], qp8[i][1] * f8[2 * i]], axis=0) for i in range(NSUB)]
        kd = jnp.concatenate(
            [x for i in range(NSUB)
             for x in (kd8[i * SUB:i * SUB + HALF, :] * f8[2 * i + 1],
                       kd8[i * SUB + HALF:(i + 1) * SUB, :])], axis=0)
        fblk = [f8[2 * i] * f8[2 * i + 1] for i in range(NSUB)]

        f01 = fblk[0] * fblk[1]
        f12 = fblk[1] * fblk[2]
        f23 = fblk[2] * fblk[3]
        f012 = f01 * fblk[2]
        f123 = fblk[1] * f23
        fall_sc[buf, 0:1, :] = f01 * f23
        qa_sc[buf] = jnp.concatenate(
            [qp[0], qp[1] * fblk[0], qp[2] * f01, qp[3] * f012], axis=0).astype(BF16)
        ke_sc[buf] = jnp.concatenate(
            [blk(kd, 0) * f123, blk(kd, 1) * f23, blk(kd, 2) * fblk[3], blk(kd, 3)],
            axis=0).astype(BF16)
        qpb = [x.astype(BF16) for x in qp]
        kdb = [blk(kd, j).astype(BF16) for j in range(NSUB)]
        kd0_f1 = (blk(kd, 0) * fblk[1]).astype(BF16)
        kd0_f12 = (blk(kd, 0) * f12).astype(BF16)
        kd1_f2 = (blk(kd, 1) * fblk[2]).astype(BF16)
        for h in range(HG_HEADS):
            hs = slice(h * HG_DK, (h + 1) * HG_DK)
            z = zero_blk
            lhs = jnp.concatenate([
                jnp.concatenate([z, qpb[1][:, hs], z, z], axis=0),
                jnp.concatenate([z, z, qpb[2][:, hs], z], axis=0),
                jnp.concatenate([z, z, z, qpb[3][:, hs]], axis=0)], axis=1)
            rhs = jnp.concatenate([
                jnp.concatenate([kdb[0][:, hs], z, z, z], axis=0),
                jnp.concatenate([kd0_f1[:, hs], kdb[1][:, hs], z, z], axis=0),
                jnp.concatenate([kd0_f12[:, hs], kd1_f2[:, hs], kdb[2][:, hs], z], axis=0)],
                axis=1)
            a_diag_h = jnp.concatenate([a_diag[i][h * SUB:(h + 1) * SUB, :] for i in range(NSUB)],
                                       axis=0)
            a = (_dot_nt(lhs, rhs)
                 + jnp.where(pair_mask, _dot_nt(lhs16[:, hs], rhs16[:, hs]), 0.0)
                 + jnp.where(diag_mask, a_diag_h, 0.0))
            a_sc[buf, h] = a.astype(BF16)

    def apply(c, buf):
        rows = pl.ds(pl.multiple_of(c * CHUNK, CHUNK), CHUNK)
        f_all = fall_sc[buf, 0:1, :]
        for h in range(HG_HEADS):
            hs = slice(h * HG_DK, (h + 1) * HG_DK)
            vb = v_ref[0, rows, hs]
            st = st_ref[h]
            o = _dot(a_sc[buf, h], vb) + _dot_nt(qa_sc[buf, :, hs], st.astype(BF16))
            st_ref[h] = st * f_all[:, hs] + _dot_tn(vb, ke_sc[buf, :, hs])
            o = o * lax.rsqrt(jnp.mean(o * o, axis=-1, keepdims=True) + RMS_EPS) * ng
            o = o * _silu(g_ref[0, rows, hs])
            o_ref[0, rows, hs] = o.astype(o_ref.dtype)

    n_pairs = HGRN_TILE // CHUNK // 2
    scores(0, 0)

    def body(j, carry):
        c = 2 * j
        apply(c, 0)
        scores(c + 1, 1)
        scores(c + 2, 0)
        apply(c + 1, 1)
        return carry

    lax.fori_loop(0, n_pairs - 1, body, 0)
    last = 2 * n_pairs - 2
    apply(last, 0)
    scores(last + 1, 1)
    apply(last + 1, 1)


def _seq_spec(tile, width):
    return pl.BlockSpec((1, tile, width), lambda b, t: (b, t, 0))


def _hgrn_selectors():
    row = lax.broadcasted_iota(jnp.int32, (CHUNK, CHUNK), 0)
    col = lax.broadcasted_iota(jnp.int32, (CHUNK, CHUNK), 1)
    suffix = ((col // HALF == row // HALF) & (col > row)).astype(BF16)
    srow = lax.broadcasted_iota(jnp.int32, (HALF * HG_DK, CHUNK), 0)
    scol = lax.broadcasted_iota(jnp.int32, (HALF * HG_DK, CHUNK), 1)
    sumsel = (srow // HG_DK == scol % HALF).astype(BF16)
    return suffix, sumsel


def _hgrn_call(lb_logits, norm_g, hq, hf, hi, hg):
    bsz, seqlen, _ = hq.shape
    suffix, sumsel = _hgrn_selectors()
    return pl.pallas_call(
        _hgrn_kernel,
        grid=(bsz, seqlen // HGRN_TILE),
        in_specs=[_resident(lb_logits.shape), _resident(norm_g.shape),
                  _resident(suffix.shape), _resident(sumsel.shape),
                  _seq_spec(HGRN_TILE, HG_FDIM), _seq_spec(HGRN_TILE, HG_FDIM),
                  _seq_spec(HGRN_TILE, HG_WIDTH), _seq_spec(HGRN_TILE, HG_WIDTH)],
        out_specs=_seq_spec(HGRN_TILE, HG_WIDTH),
        out_shape=jax.ShapeDtypeStruct((bsz, seqlen, HG_WIDTH), BF16),
        scratch_shapes=[
            pltpu.VMEM((HG_HEADS, HG_DV, HG_DK), F32),
            pltpu.VMEM((2, NSUB, HG_HEADS * SUB, HALF * HG_DK), BF16),
            pltpu.VMEM((2, HG_HEADS, CHUNK, CHUNK), BF16),
            pltpu.VMEM((2, CHUNK, HG_FDIM), BF16),
            pltpu.VMEM((2, CHUNK, HG_FDIM), BF16),
            pltpu.VMEM((2, SUBLANES, HG_FDIM), F32),
        ],
        compiler_params=pltpu.CompilerParams(
            dimension_semantics=("arbitrary", "arbitrary"), vmem_limit_bytes=V7X_VMEM_LIMIT_BYTES),
        name="hgrn2_mixer",
    )(lb_logits, norm_g, suffix, sumsel, hq, hf, hi, hg)


def _ssd_kernel(cw_ref, cb_ref, dtb_ref, alog_ref, dexp_ref, ng_ref,
                expand_ref, tri_ref, upper_ref,
                z_ref, xbc_ref, dt_ref, o_ref, ht_ref, xbuf, xc_sc):
    @pl.when(pl.program_id(1) == 0)
    def _():
        ht_ref[...] = jnp.zeros_like(ht_ref)
        xbuf[...] = jnp.zeros_like(xbuf)

    dexp = dexp_ref[...]
    ng = ng_ref[...]
    rate = -jnp.exp(alog_ref[...]) * LOG2_E

    for r0 in range(0, SEQ_TILE, CONV_ROWS):
        if r0 == 0:
            win = jnp.concatenate([xbuf[...], xbc_ref[0, 0:CONV_ROWS, :]], axis=0)
        else:
            win = xbc_ref[0, r0 - SUBLANES:r0 + CONV_ROWS, :]
        conv = cb_ref[...] + cw_ref[SSD_CONV - 1:SSD_CONV, :] * win[SUBLANES:, :]
        for d in range(1, SSD_CONV):
            shifted = pltpu.roll(win, d, axis=0)[SUBLANES:, :]
            conv = conv + cw_ref[SSD_CONV - 1 - d:SSD_CONV - d, :] * shifted
        xc_sc[r0:r0 + CONV_ROWS, :] = _silu(conv)
    xbuf[...] = xbc_ref[0, SEQ_TILE - SUBLANES:SEQ_TILE, :]

    dt = _softplus(dt_ref[0] + dtb_ref[...])
    a = dt * rate
    expand = expand_ref[...]
    acum_h = _dot_sel(tri_ref[...], a)
    acum_all = _dot_x_sel(acum_h, expand)
    acum_t_all = _dot_x_sel(a.T, upper_ref[...])
    dtx_all = xc_sc[:, 0:SSD_WIDTH] * _dot_x_sel(dt, expand)

    row = lax.broadcasted_iota(jnp.int32, (SSD_Q, LANES), 0)
    lane = lax.broadcasted_iota(jnp.int32, (SSD_Q, LANES), 1)
    key = lane % SSD_HEADDIM
    first_head = lane < SSD_HEADDIM
    krow = lax.broadcasted_iota(jnp.int32, (SSD_HEADDIM, LANES), 1) < SSD_HEADDIM

    for c in range(SEQ_TILE // SSD_Q):
        rows = slice(c * SSD_Q, (c + 1) * SSD_Q)
        acum = acum_all[rows, :]
        acum_t = acum_t_all[:, rows]
        xs = xc_sc[rows, 0:SSD_WIDTH]
        dtx = dtx_all[rows, :]
        a_last = acum[SSD_Q - 1:SSD_Q, :]
        dtx_end = (dtx * jnp.exp2(a_last - acum)).astype(BF16)
        dtxb = dtx.astype(BF16)
        e_cum = jnp.exp2(acum)
        e_last = jnp.exp2(a_last)

        ys = []
        for g in range(SSD_GROUPS):
            gs = slice(g * SSD_GROUP_W, (g + 1) * SSD_GROUP_W)
            b0 = SSD_WIDTH + g * SSD_STATE
            c0 = SSD_WIDTH + SSD_GROUPS * SSD_STATE + g * SSD_STATE
            bm = xc_sc[rows, b0:b0 + SSD_STATE].astype(BF16)
            cm = xc_sc[rows, c0:c0 + SSD_STATE].astype(BF16)
            ht = ht_ref[g]
            y_grp = _dot(cm, ht.astype(BF16)) * e_cum[:, gs]
            y_pairs = [None] * (SSD_HPG // 2)
            for half in range(SSD_Q // SSD_HEADDIM):
                s0 = half * SSD_HEADDIM
                bm_half = bm[s0:s0 + SSD_HEADDIM, :]
                cb2 = _dot_nt(cm, jnp.concatenate([bm_half, bm_half], axis=0))
                causal2 = row >= key + s0
                for pr in range(SSD_HPG // 2):
                    h = g * SSD_HPG + 2 * pr
                    lanes = slice(h * SSD_HEADDIM, (h + 2) * SSD_HEADDIM)
                    keys_a = acum_t[h:h + 1, s0:s0 + SSD_HEADDIM]
                    keys_b = acum_t[h + 1:h + 2, s0:s0 + SSD_HEADDIM]
                    keys_t = jnp.where(first_head[0:1, :],
                                       jnp.concatenate([keys_a, keys_a], axis=1),
                                       jnp.concatenate([keys_b, keys_b], axis=1))
                    seg = acum[:, lanes] - keys_t
                    lmat = jnp.exp2(jnp.where(causal2, seg, NEG_BIG))
                    x_half = dtxb[s0:s0 + SSD_HEADDIM, lanes]
                    zero = jnp.zeros_like(x_half)
                    rhs = jnp.concatenate([jnp.where(krow, x_half, zero),
                                           jnp.where(krow, zero, x_half)], axis=0)
                    part = _dot((cb2 * lmat).astype(BF16), rhs)
                    y_pairs[pr] = part if y_pairs[pr] is None else y_pairs[pr] + part
            ys.append(jnp.concatenate(y_pairs, axis=1) + y_grp)
            ht_ref[g] = ht * e_last[:, gs] + _dot_tn(bm, dtx_end[:, gs])
        y = jnp.concatenate(ys, axis=1) + dexp * xs
        y = y * _silu(z_ref[0, rows, :])
        y = y * lax.rsqrt(jnp.mean(y * y, axis=-1, keepdims=True) + RMS_EPS) * ng
        o_ref[0, rows, :] = y.astype(o_ref.dtype)


def _ssd_selectors():
    hrow = lax.broadcasted_iota(jnp.int32, (DT_PAD, SSD_WIDTH), 0)
    hcol = lax.broadcasted_iota(jnp.int32, (DT_PAD, SSD_WIDTH), 1)
    expand = (hrow == hcol // SSD_HEADDIM).astype(BF16)
    row = lax.broadcasted_iota(jnp.int32, (SEQ_TILE, SEQ_TILE), 0)
    col = lax.broadcasted_iota(jnp.int32, (SEQ_TILE, SEQ_TILE), 1)
    same_chunk = row // SSD_Q == col // SSD_Q
    tri = (same_chunk & (col <= row)).astype(BF16)
    upper = (same_chunk & (row <= col)).astype(BF16)
    return expand, tri, upper


def _ssd_call(conv_w, conv_b, dt_bias, a_log, d_exp, norm_g, sz, sxbc, sdt):
    bsz, seqlen, _ = sz.shape
    params = (conv_w, conv_b, dt_bias, a_log, d_exp, norm_g) + _ssd_selectors()
    return pl.pallas_call(
        _ssd_kernel,
        grid=(bsz, seqlen // SEQ_TILE),
        in_specs=[_resident(p.shape) for p in params]
        + [_seq_spec(SEQ_TILE, SSD_WIDTH), _seq_spec(SEQ_TILE, SSD_CONV_CH),
           _seq_spec(SEQ_TILE, DT_PAD)],
        out_specs=_seq_spec(SEQ_TILE, SSD_WIDTH),
        out_shape=jax.ShapeDtypeStruct((bsz, seqlen, SSD_WIDTH), BF16),
        scratch_shapes=[
            pltpu.VMEM((SSD_GROUPS, SSD_STATE, SSD_GROUP_W), F32),
            pltpu.VMEM((SUBLANES, SSD_CONV_CH), F32),
            pltpu.VMEM((SEQ_TILE, SSD_CONV_CH), F32),
        ],
        compiler_params=pltpu.CompilerParams(
            dimension_semantics=("arbitrary", "arbitrary"), vmem_limit_bytes=V7X_VMEM_LIMIT_BYTES),
        name="ssd_mixer",
    )(*params, sz, sxbc, sdt)


def _post_kernel(oh_ref, os_ref, x1_ref, p_ref, wo_ref, g2_ref, b2_ref,
                 w2i_ref, w2o_ref, g3_ref, b3_ref, wg_ref, wp_ref, g4_ref, b4_ref, out_ref):
    x2 = []
    for rows in _SUBTILES:
        mix = (_dot(oh_ref[rows, :], wo_ref[0:HG_WIDTH, :])
               + _dot(os_ref[rows, :], wo_ref[HG_WIDTH:, :]))
        x2.append(_layer_norm(DEEPNORM_ALPHA * x1_ref[rows, :] + mix, g2_ref[...], b2_ref[...]))
    x3 = [_ffn_ln(x, w2i_ref, w2o_ref, g3_ref[...], b3_ref[...]) for x in x2]
    for rows, x in zip(_SUBTILES, x3):
        gate = _sigmoid(_dot(x.astype(BF16), wg_ref[...]))
        ple = gate * _dot(p_ref[rows, :].astype(BF16), wp_ref[...])
        out_ref[rows, :] = _layer_norm(DEEPNORM_ALPHA * x + ple, g4_ref[...], b4_ref[...])


def _post_call(oh, osd, x1, p2d, wo, g2, b2, w2i, w2o, g3, b3, wg, wp, g4, b4):
    n = x1.shape[0]
    weights = (wo, g2, b2, w2i, w2o, g3, b3, wg, wp, g4, b4)
    return pl.pallas_call(
        _post_kernel,
        grid=(n // TOKEN_TILE,),
        in_specs=[_rows(HG_WIDTH), _rows(SSD_WIDTH), _rows(D_MODEL), _rows(PLE_DIM)]
        + [_resident(w.shape) for w in weights],
        out_specs=_rows(D_MODEL),
        out_shape=jax.ShapeDtypeStruct((n, D_MODEL), F32),
        compiler_params=pltpu.CompilerParams(
            dimension_semantics=("arbitrary",), vmem_limit_bytes=V7X_VMEM_LIMIT_BYTES),
        name="post_outproj_ffn_ple",
    )(oh, osd, x1, p2d, *weights)


def _row(v):
    return v.reshape(1, -1).astype(F32)


def kernel(x, p, ffn1_w_in, ffn1_w_out, ln1_g, ln1_b, w_in_mix, hgrn_lb_logits, hgrn_norm_g,
           ssd_conv_w, ssd_conv_b, ssd_dt_bias, ssd_a_log, ssd_d, ssd_norm_g, w_out_mix,
           ln2_g, ln2_b, ffn2_w_in, ffn2_w_out, ln3_g, ln3_b, ple_w_proj, ple_w_gate,
           ln4_g, ln4_b):
    bsz, seqlen, _ = x.shape
    n = bsz * seqlen
    assert DEPTH == 1 and n % TOKEN_TILE == 0
    assert seqlen % SEQ_TILE == 0 and seqlen % HGRN_TILE == 0
    h = x.reshape(n, D_MODEL)
    for i in range(DEPTH):
        pad = MIX_COLS_PADDED - w_in_mix.shape[-1]
        wmix = jnp.pad(w_in_mix[i], ((0, 0), (0, pad))).astype(BF16)
        x1, x1b = _pre_call(h, ffn1_w_in[i].astype(BF16), ffn1_w_out[i].astype(BF16),
                            _row(ln1_g[i]), _row(ln1_b[i]))
        hq, hf, hi, hg, sz, sxbc, sdt = _proj_call(x1b, wmix)

        def seq(t):
            return t.reshape(bsz, seqlen, t.shape[-1])

        o_h = _hgrn_call(hgrn_lb_logits.astype(F32), _row(hgrn_norm_g[i]),
                         seq(hq), seq(hf), seq(hi), seq(hg))
        head_pad = DT_PAD - SSD_HEADS
        o_s = _ssd_call(
            ssd_conv_w[i].astype(F32), _row(ssd_conv_b[i]),
            jnp.pad(_row(ssd_dt_bias[i]), ((0, 0), (0, head_pad))),
            jnp.pad(_row(ssd_a_log[i]), ((0, 0), (0, head_pad))),
            _row(jnp.repeat(ssd_d[i], SSD_HEADDIM)), _row(ssd_norm_g[i]),
            seq(sz), seq(sxbc), seq(sdt))
        h = _post_call(
            o_h.reshape(n, HG_WIDTH), o_s.reshape(n, SSD_WIDTH), x1, p[i].reshape(n, PLE_DIM),
            w_out_mix[i].astype(BF16), _row(ln2_g[i]), _row(ln2_b[i]),
            ffn2_w_in[i].astype(BF16), ffn2_w_out[i].astype(BF16), _row(ln3_g[i]), _row(ln3_b[i]),
            ple_w_gate[i].astype(BF16), ple_w_proj[i].astype(BF16), _row(ln4_g[i]), _row(ln4_b[i]))
    return h.reshape(bsz, seqlen, D_MODEL)
```

```python
import jax
import jax.numpy as jnp
from jax import lax
from jax.experimental import pallas as pl
from jax.experimental.pallas import tpu as pltpu

F32 = jnp.float32
BF16 = jnp.bfloat16

DEPTH = 1
D_MODEL = 1024
D_FF = 2816
CHUNK = 64
PLE_DIM = 256
HG_HEADS = 4
HG_DK = 128
HG_DV = 128
HG_WIDTH = HG_HEADS * HG_DV
HG_FDIM = HG_HEADS * HG_DK
SSD_WIDTH = 512
SSD_HEADDIM = 64
SSD_HEADS = SSD_WIDTH // SSD_HEADDIM
SSD_GROUPS = 2
SSD_HPG = SSD_HEADS // SSD_GROUPS
SSD_STATE = 128
SSD_CONV = 4
SSD_CONV_CH = SSD_WIDTH + 2 * SSD_GROUPS * SSD_STATE
SSD_GROUP_W = SSD_HPG * SSD_HEADDIM
DEEPNORM_ALPHA = (2.0 * DEPTH) ** 0.25
LN_EPS = 1e-5
RMS_EPS = 1e-6
LOG2_E = 1.4426950408889634

LANES = 128
SUBLANES = 8
V7X_MXU_DIM = 256
V7X_VMEM_LIMIT_BYTES = 56 * 1024 * 1024

DT_PAD = LANES
MIX_COLS_PADDED = 2 * HG_FDIM + 2 * HG_WIDTH + SSD_WIDTH + SSD_CONV_CH + DT_PAD
TOKEN_TILE = 1024
SUBTILE = 256
_SUBTILES = tuple(slice(r, r + SUBTILE) for r in range(0, TOKEN_TILE, SUBTILE))
FF_SPLITS = (0, (D_FF // V7X_MXU_DIM + 1) // 2 * V7X_MXU_DIM, D_FF)
SEQ_TILE = 256
HGRN_TILE = 2048
SUB = 16
NSUB = CHUNK // SUB
HALF = SUB // 2
NEG_BIG = -1e30
SSD_Q = 128
CONV_ROWS = 256


def _sigmoid(x):
    return 1.0 / (1.0 + jnp.exp(-x))


def _silu(x):
    return x * _sigmoid(x)


def _softplus(x):
    return jnp.maximum(x, 0.0) + jnp.log1p(jnp.exp(-jnp.abs(x)))


def _dot(a, b):
    return jnp.dot(a, b, preferred_element_type=F32)


def _dot_nt(a, b):
    return lax.dot_general(a, b, (((1,), (1,)), ((), ())), preferred_element_type=F32)


def _dot_tn(a, b):
    return lax.dot_general(a, b, (((0,), (0,)), ((), ())), preferred_element_type=F32)


def _split3(x):
    hi = x.astype(BF16)
    r1 = x - hi.astype(F32)
    mid = r1.astype(BF16)
    lo = (r1 - mid.astype(F32)).astype(BF16)
    return hi, mid, lo


def _dot_sel(sel, x):
    hi, mid, lo = _split3(x)
    return _dot(sel, hi) + _dot(sel, mid) + _dot(sel, lo)


def _dot_x_sel(x, sel):
    hi, mid, lo = _split3(x)
    return _dot(hi, sel) + _dot(mid, sel) + _dot(lo, sel)


def _layer_norm(y, g, b):
    mu = jnp.mean(y, axis=-1, keepdims=True)
    d = y - mu
    var = jnp.mean(d * d, axis=-1, keepdims=True)
    return d * lax.rsqrt(var + LN_EPS) * g + b


def _ffn_ln(x, w_in_ref, w_out_ref, g, b):
    xb = x.astype(BF16)
    acc = None
    for c0, c1 in zip(FF_SPLITS[:-1], FF_SPLITS[1:]):
        gate = _dot(xb, w_in_ref[:, c0:c1])
        up = _dot(xb, w_in_ref[:, D_FF + c0:D_FF + c1])
        act = (_silu(gate) * up).astype(BF16)
        part = _dot(act, w_out_ref[c0:c1, :])
        acc = part if acc is None else acc + part
    return _layer_norm(DEEPNORM_ALPHA * x + 0.5 * acc, g, b)


def _pre_kernel(x_ref, w1i_ref, w1o_ref, g1_ref, b1_ref, x1_ref, x1b_ref):
    for rows in _SUBTILES:
        x1 = _ffn_ln(x_ref[rows, :], w1i_ref, w1o_ref, g1_ref[...], b1_ref[...])
        x1_ref[rows, :] = x1
        x1b_ref[rows, :] = x1.astype(BF16)


def _resident(shape):
    return pl.BlockSpec(shape, lambda *_: (0,) * len(shape), pipeline_mode=pl.Buffered(1))


def _rows(width):
    return pl.BlockSpec((TOKEN_TILE, width), lambda i: (i, 0))


def _pre_call(x2d, w1i, w1o, g1, b1):
    n = x2d.shape[0]
    return pl.pallas_call(
        _pre_kernel,
        grid=(n // TOKEN_TILE,),
        in_specs=[_rows(D_MODEL), _resident(w1i.shape), _resident(w1o.shape),
                  _resident(g1.shape), _resident(b1.shape)],
        out_specs=[_rows(D_MODEL), _rows(D_MODEL)],
        out_shape=[jax.ShapeDtypeStruct((n, D_MODEL), F32),
                   jax.ShapeDtypeStruct((n, D_MODEL), BF16)],
        compiler_params=pltpu.CompilerParams(
            dimension_semantics=("arbitrary",), vmem_limit_bytes=V7X_VMEM_LIMIT_BYTES),
        name="pre_ffn",
    )(x2d, w1i, w1o, g1, b1)


_MIX_WIDTHS = (HG_FDIM, HG_FDIM, HG_WIDTH, HG_WIDTH, SSD_WIDTH, SSD_CONV_CH, DT_PAD)
_MIX_DTYPES = (F32, F32, BF16, F32, F32, F32, F32)


def _proj_kernel(x1b_ref, wmix_ref, *out_refs):
    c = 0
    for ref in out_refs:
        w = ref.shape[-1]
        ref[...] = _dot(x1b_ref[...], wmix_ref[:, c:c + w]).astype(ref.dtype)
        c += w


def _proj_call(x1b, wmix):
    n = x1b.shape[0]
    return pl.pallas_call(
        _proj_kernel,
        grid=(n // TOKEN_TILE,),
        in_specs=[_rows(D_MODEL), _resident(wmix.shape)],
        out_specs=[_rows(w) for w in _MIX_WIDTHS],
        out_shape=[jax.ShapeDtypeStruct((n, w), d) for w, d in zip(_MIX_WIDTHS, _MIX_DTYPES)],
        compiler_params=pltpu.CompilerParams(
            dimension_semantics=("arbitrary",), vmem_limit_bytes=V7X_VMEM_LIMIT_BYTES),
        name="mix_inproj",
    )(x1b, wmix)


def _hgrn_kernel(lbl_ref, ng_ref, suffix_ref, sumsel_ref, q_ref, f_ref, v_ref, g_ref, o_ref,
                 st_ref, f_sc, kk_sc, p_sc, a_sc, qa_sc, ke_sc, fall_sc):
    @pl.when(pl.program_id(1) == 0)
    def _():
        st_ref[...] = jnp.zeros_like(st_ref)

    logits = lbl_ref[...]
    ex = jnp.exp(logits - jnp.max(logits, axis=0, keepdims=True))
    lb = ex[0:1, :] / jnp.sum(ex, axis=0, keepdims=True)
    ng = ng_ref[...]

    row = lax.broadcasted_iota(jnp.int32, (CHUNK, CHUNK), 0)
    col = lax.broadcasted_iota(jnp.int32, (CHUNK, CHUNK), 1)
    diag_mask = (col // HALF == row // HALF) & (col <= row)
    pair_mask = col // SUB == row // SUB
    tloc = lax.broadcasted_iota(jnp.int32, (HALF, HG_DK), 0)
    zero_blk = jnp.zeros((SUB, HG_DK), BF16)

    def blk(x, i):
        return x[i * SUB:(i + 1) * SUB, :]

    def scores(c, buf):
        rows = pl.ds(pl.multiple_of(c * CHUNK, CHUNK), CHUNK)
        f = lb + (1.0 - lb) * _sigmoid(f_ref[0, rows, :])
        kk = 1.0 - f
        qq = _silu(q_ref[0, rows, :])
        f_sc[buf] = f
        kk_sc[buf] = kk
        d8 = jnp.exp2(_dot_sel(suffix_ref[...], jnp.log2(f)))
        kd8 = kk * d8
        f8 = [f[b * HALF:b * HALF + 1, :] * d8[b * HALF:b * HALF + 1, :] for b in range(2 * NSUB)]

        qp8 = []
        a_diag = []
        hsl = [slice(h * HG_DK, (h + 1) * HG_DK) for h in range(HG_HEADS)]
        for i in range(NSUB):
            q_lo = [qq[i * SUB:i * SUB + HALF, hs] for hs in hsl]
            q_hi = [qq[i * SUB + HALF:(i + 1) * SUB, hs] for hs in hsl]
            w_lo, w_hi = list(q_lo), list(q_hi)
            for s in range(HALF - 1, -1, -1):
                r_lo = i * SUB + s
                r_hi = r_lo + HALF
                keep = tloc >= s
                for h, hs in enumerate(hsl):
                    k_lo = jnp.broadcast_to(kk_sc[buf, r_lo:r_lo + 1, hs], (HALF, HG_DK))
                    k_hi = jnp.broadcast_to(kk_sc[buf, r_hi:r_hi + 1, hs], (HALF, HG_DK))
                    piece = jnp.concatenate([w_lo[h] * k_lo, w_hi[h] * k_hi], axis=0).astype(BF16)
                    p_sc[buf, i, h * SUB:(h + 1) * SUB, s * HG_DK:(s + 1) * HG_DK] = piece
                    w_lo[h] = w_lo[h] * jnp.broadcast_to(f_sc[buf, r_lo:r_lo + 1, hs], (HALF, HG_DK))
                    w_hi[h] = w_hi[h] * jnp.broadcast_to(f_sc[buf, r_hi:r_hi + 1, hs], (HALF, HG_DK))
                    if s > 0:
                        w_lo[h] = jnp.where(keep, w_lo[h], q_lo[h])
                        w_hi[h] = jnp.where(keep, w_hi[h], q_hi[h])
            qp8.append((jnp.concatenate(w_lo, axis=1), jnp.concatenate(w_hi, axis=1)))
            a_diag.append(_dot(p_sc[buf, i], sumsel_ref[...]))

        zero8 = jnp.zeros((HALF, HG_FDIM), F32)
        lhs16 = jnp.concatenate([x for i in range(NSUB) for x in (zero8, qp8[i][1])],
                                axis=0).astype(BF16)
        rhs16 = jnp.concatenate(
            [x for i in range(NSUB) for x in (kd8[i * SUB:i * SUB + HALF, :], zero8)],
            axis=0).astype(BF16)
        qp = [jnp.concatenate([qp8[i][0], qp8[i][1] * f8[2 * i]], axis=0) for i in range(NSUB)]
        kd = jnp.concatenate(
            [x for i in range(NSUB)
             for x in (kd8[i * SUB:i * SUB + HALF, :] * f8[2 * i + 1],
                       kd8[i * SUB + HALF:(i + 1) * SUB, :])], axis=0)
        fblk = [f8[2 * i] * f8[2 * i + 1] for i in range(NSUB)]

        f01 = fblk[0] * fblk[1]
        f12 = fblk[1] * fblk[2]
        f23 = fblk[2] * fblk[3]
        f012 = f01 * fblk[2]
        f123 = fblk[1] * f23
        fall_sc[buf, 0:1, :] = f01 * f23
        qa_sc[buf] = jnp.concatenate(
            [qp[0], qp[1] * fblk[0], qp[2] * f01, qp[3] * f012], axis=0).astype(BF16)
        ke_sc[buf] = jnp.concatenate(
            [blk(kd, 0) * f123, blk(kd, 1) * f23, blk(kd, 2) * fblk[3], blk(kd, 3)],
            axis=0).astype(BF16)
        qpb = [x.astype(BF16) for x in qp]
        kdb = [blk(kd, j).astype(BF16) for j in range(NSUB)]
        kd0_f1 = (blk(kd, 0) * fblk[1]).astype(BF16)
        kd0_f12 = (blk(kd, 0) * f12).astype(BF16)
        kd1_f2 = (blk(kd, 1) * fblk[2]).astype(BF16)
        for h in range(HG_HEADS):
            hs = slice(h * HG_DK, (h + 1) * HG_DK)
            z = zero_blk
            lhs = jnp.concatenate([
                jnp.concatenate([z, qpb[1][:, hs], z, z], axis=0),
                jnp.concatenate([z, z, qpb[2][:, hs], z], axis=0),
                jnp.concatenate([z, z, z, qpb[3][:, hs]], axis=0)], axis=1)
            rhs = jnp.concatenate([
                jnp.concatenate([kdb[0][:, hs], z, z, z], axis=0),
                jnp.concatenate([kd0_f1[:, hs], kdb[1][:, hs], z, z], axis=0),
                jnp.concatenate([kd0_f12[:, hs], kd1_f2[:, hs], kdb[2][:, hs], z], axis=0)],
                axis=1)
            a_diag_h = jnp.concatenate([a_diag[i][h * SUB:(h + 1) * SUB, :] for i in range(NSUB)],
                                       axis=0)
            a = (_dot_nt(lhs, rhs)
                 + jnp.where(pair_mask, _dot_nt(lhs16[:, hs], rhs16[:, hs]), 0.0)
                 + jnp.where(diag_mask, a_diag_h, 0.0))
            a_sc[buf, h] = a.astype(BF16)

    def apply(c, buf):
        rows = pl.ds(pl.multiple_of(c * CHUNK, CHUNK), CHUNK)
        f_all = fall_sc[buf, 0:1, :]
        for h in range(HG_HEADS):
            hs = slice(h * HG_DK, (h + 1) * HG_DK)
            vb = v_ref[0, rows, hs]
            st = st_ref[h]
            o = _dot(a_sc[buf, h], vb) + _dot_nt(qa_sc[buf, :, hs], st.astype(BF16))
            st_ref[h] = st * f_all[:, hs] + _dot_tn(vb, ke_sc[buf, :, hs])
            o = o * lax.rsqrt(jnp.mean(o * o, axis=-1, keepdims=True) + RMS_EPS) * ng
            o = o * _silu(g_ref[0, rows, hs])
            o_ref[0, rows, hs] = o.astype(o_ref.dtype)

    n_pairs = HGRN_TILE // CHUNK // 2
    scores(0, 0)

    def body(j, carry):
        c = 2 * j
        apply(c, 0)
        scores(c + 1, 1)
        scores(c + 2, 0)
        apply(c + 1, 1)
        return carry

    lax.fori_loop(0, n_pairs - 1, body, 0)
    last = 2 * n_pairs - 2
    apply(last, 0)
    scores(last + 1, 1)
    apply(last + 1, 1)


def _seq_spec(tile, width):
    return pl.BlockSpec((1, tile, width), lambda b, t: (b, t, 0))


def _hgrn_selectors():
    row = lax.broadcasted_iota(jnp.int32, (CHUNK, CHUNK), 0)
    col = lax.broadcasted_iota(jnp.int32, (CHUNK, CHUNK), 1)
    suffix = ((col // HALF == row // HALF) & (col > row)).astype(BF16)
    srow = lax.broadcasted_iota(jnp.int32, (HALF * HG_DK, CHUNK), 0)
    scol = lax.broadcasted_iota(jnp.int32, (HALF * HG_DK, CHUNK), 1)
    sumsel = (srow // HG_DK == scol % HALF).astype(BF16)
    return suffix, sumsel


def _hgrn_call(lb_logits, norm_g, hq, hf, hi, hg):
    bsz, seqlen, _ = hq.shape
    suffix, sumsel = _hgrn_selectors()
    return pl.pallas_call(
        _hgrn_kernel,
        grid=(bsz, seqlen // HGRN_TILE),
        in_specs=[_resident(lb_logits.shape), _resident(norm_g.shape),
                  _resident(suffix.shape), _resident(sumsel.shape),
                  _seq_spec(HGRN_TILE, HG_FDIM), _seq_spec(HGRN_TILE, HG_FDIM),
                  _seq_spec(HGRN_TILE, HG_WIDTH), _seq_spec(HGRN_TILE, HG_WIDTH)],
        out_specs=_seq_spec(HGRN_TILE, HG_WIDTH),
        out_shape=jax.ShapeDtypeStruct((bsz, seqlen, HG_WIDTH), BF16),
        scratch_shapes=[
            pltpu.VMEM((HG_HEADS, HG_DV, HG_DK), F32),
            pltpu.VMEM((2, CHUNK, HG_FDIM), F32),
            pltpu.VMEM((2, CHUNK, HG_FDIM), F32),
            pltpu.VMEM((2, NSUB, HG_HEADS * SUB, HALF * HG_DK), BF16),
            pltpu.VMEM((2, HG_HEADS, CHUNK, CHUNK), BF16),
            pltpu.VMEM((2, CHUNK, HG_FDIM), BF16),
            pltpu.VMEM((2, CHUNK, HG_FDIM), BF16),
            pltpu.VMEM((2, SUBLANES, HG_FDIM), F32),
        ],
        compiler_params=pltpu.CompilerParams(
            dimension_semantics=("arbitrary", "arbitrary"), vmem_limit_bytes=V7X_VMEM_LIMIT_BYTES),
        name="hgrn2_mixer",
    )(lb_logits, norm_g, suffix, sumsel, hq, hf, hi, hg)


def _ssd_kernel(cw_ref, cb_ref, dtb_ref, alog_ref, dexp_ref, ng_ref,
                expand_ref, tri_ref, upper_ref,
                z_ref, xbc_ref, dt_ref, o_ref, ht_ref, xbuf, xc_sc, acum_sc, dtx_sc):
    @pl.when(pl.program_id(1) == 0)
    def _():
        ht_ref[...] = jnp.zeros_like(ht_ref)
        xbuf[...] = jnp.zeros_like(xbuf)

    dexp = dexp_ref[...]
    ng = ng_ref[...]
    rate = -jnp.exp(alog_ref[...]) * LOG2_E

    for r0 in range(0, SEQ_TILE, CONV_ROWS):
        if r0 == 0:
            win = jnp.concatenate([xbuf[...], xbc_ref[0, 0:CONV_ROWS, :]], axis=0)
        else:
            win = xbc_ref[0, r0 - SUBLANES:r0 + CONV_ROWS, :]
        conv = cb_ref[...] + cw_ref[SSD_CONV - 1:SSD_CONV, :] * win[SUBLANES:, :]
        for d in range(1, SSD_CONV):
            shifted = pltpu.roll(win, d, axis=0)[SUBLANES:, :]
            conv = conv + cw_ref[SSD_CONV - 1 - d:SSD_CONV - d, :] * shifted
        xc_sc[r0:r0 + CONV_ROWS, :] = _silu(conv)
    xbuf[...] = xbc_ref[0, SEQ_TILE - SUBLANES:SEQ_TILE, :]

    dt = _softplus(dt_ref[0] + dtb_ref[...])
    a = dt * rate
    expand = expand_ref[...]
    acum_h = _dot_sel(tri_ref[...], a)
    acum_sc[...] = _dot_x_sel(acum_h, expand)
    acum_t_all = _dot_x_sel(a.T, upper_ref[...])
    dtx_sc[...] = xc_sc[:, 0:SSD_WIDTH] * _dot_x_sel(dt, expand)

    row = lax.broadcasted_iota(jnp.int32, (SSD_Q, LANES), 0)
    lane = lax.broadcasted_iota(jnp.int32, (SSD_Q, LANES), 1)
    key = lane % SSD_HEADDIM
    first_head = lane < SSD_HEADDIM
    krow = lax.broadcasted_iota(jnp.int32, (SSD_HEADDIM, LANES), 1) < SSD_HEADDIM

    for c in range(SEQ_TILE // SSD_Q):
        rows = slice(c * SSD_Q, (c + 1) * SSD_Q)
        acum = acum_sc[rows, :]
        acum_t = acum_t_all[:, rows]
        xs = xc_sc[rows, 0:SSD_WIDTH]
        dtx = dtx_sc[rows, :]
        a_last = acum[SSD_Q - 1:SSD_Q, :]
        dtx_end = (dtx * jnp.exp2(a_last - acum)).astype(BF16)
        dtxb = dtx.astype(BF16)
        e_cum = jnp.exp2(acum)
        e_last = jnp.exp2(a_last)

        ys = []
        for g in range(SSD_GROUPS):
            gs = slice(g * SSD_GROUP_W, (g + 1) * SSD_GROUP_W)
            b0 = SSD_WIDTH + g * SSD_STATE
            c0 = SSD_WIDTH + SSD_GROUPS * SSD_STATE + g * SSD_STATE
            bm = xc_sc[rows, b0:b0 + SSD_STATE].astype(BF16)
            cm = xc_sc[rows, c0:c0 + SSD_STATE].astype(BF16)
            ht = ht_ref[g]
            y_grp = _dot(cm, ht.astype(BF16)) * e_cum[:, gs]
            y_pairs = [None] * (SSD_HPG // 2)
            for half in range(SSD_Q // SSD_HEADDIM):
                s0 = half * SSD_HEADDIM
                bm_half = bm[s0:s0 + SSD_HEADDIM, :]
                cb2 = _dot_nt(cm, jnp.concatenate([bm_half, bm_half], axis=0))
                causal2 = row >= key + s0
                for pr in range(SSD_HPG // 2):
                    h = g * SSD_HPG + 2 * pr
                    lanes = slice(h * SSD_HEADDIM, (h + 2) * SSD_HEADDIM)
                    keys_a = acum_t[h:h + 1, s0:s0 + SSD_HEADDIM]
                    keys_b = acum_t[h + 1:h + 2, s0:s0 + SSD_HEADDIM]
                    keys_t = jnp.where(first_head[0:1, :],
                                       jnp.concatenate([keys_a, keys_a], axis=1),
                                       jnp.concatenate([keys_b, keys_b], axis=1))
                    seg = acum[:, lanes] - keys_t
                    lmat = jnp.exp2(jnp.where(causal2, seg, NEG_BIG))
                    x_half = dtxb[s0:s0 + SSD_HEADDIM, lanes]
                    zero = jnp.zeros_like(x_half)
                    rhs = jnp.concatenate([jnp.where(krow, x_half, zero),
                                           jnp.where(krow, zero, x_half)], axis=0)
                    part = _dot((cb2 * lmat).astype(BF16), rhs)
                    y_pairs[pr] = part if y_pairs[pr] is None else y_pairs[pr] + part
            ys.append(jnp.concatenate(y_pairs, axis=1) + y_grp)
            ht_ref[g] = ht * e_last[:, gs] + _dot_tn(bm, dtx_end[:, gs])
        y = jnp.concatenate(ys, axis=1) + dexp * xs
        y = y * _silu(z_ref[0, rows, :])
        y = y * lax.rsqrt(jnp.mean(y * y, axis=-1, keepdims=True) + RMS_EPS) * ng
        o_ref[0, rows, :] = y.astype(o_ref.dtype)


def _ssd_selectors():
    hrow = lax.broadcasted_iota(jnp.int32, (DT_PAD, SSD_WIDTH), 0)
    hcol = lax.broadcasted_iota(jnp.int32, (DT_PAD, SSD_WIDTH), 1)
    expand = (hrow == hcol // SSD_HEADDIM).astype(BF16)
    row = lax.broadcasted_iota(jnp.int32, (SEQ_TILE, SEQ_TILE), 0)
    col = lax.broadcasted_iota(jnp.int32, (SEQ_TILE, SEQ_TILE), 1)
    same_chunk = row // SSD_Q == col // SSD_Q
    tri = (same_chunk & (col <= row)).astype(BF16)
    upper = (same_chunk & (row <= col)).astype(BF16)
    return expand, tri, upper


def _ssd_call(conv_w, conv_b, dt_bias, a_log, d_exp, norm_g, sz, sxbc, sdt):
    bsz, seqlen, _ = sz.shape
    params = (conv_w, conv_b, dt_bias, a_log, d_exp, norm_g) + _ssd_selectors()
    return pl.pallas_call(
        _ssd_kernel,
        grid=(bsz, seqlen // SEQ_TILE),
        in_specs=[_resident(p.shape) for p in params]
        + [_seq_spec(SEQ_TILE, SSD_WIDTH), _seq_spec(SEQ_TILE, SSD_CONV_CH),
           _seq_spec(SEQ_TILE, DT_PAD)],
        out_specs=_seq_spec(SEQ_TILE, SSD_WIDTH),
        out_shape=jax.ShapeDtypeStruct((bsz, seqlen, SSD_WIDTH), BF16),
        scratch_shapes=[
            pltpu.VMEM((SSD_GROUPS, SSD_STATE, SSD_GROUP_W), F32),
            pltpu.VMEM((SUBLANES, SSD_CONV_CH), F32),
            pltpu.VMEM((SEQ_TILE, SSD_CONV_CH), F32),
            pltpu.VMEM((SEQ_TILE, SSD_WIDTH), F32),
            pltpu.VMEM((SEQ_TILE, SSD_WIDTH), F32),
        ],
        compiler_params=pltpu.CompilerParams(
            dimension_semantics=("arbitrary", "arbitrary"), vmem_limit_bytes=V7X_VMEM_LIMIT_BYTES),
        name="ssd_mixer",
    )(*params, sz, sxbc, sdt)


def _post_kernel(oh_ref, os_ref, x1_ref, p_ref, wo_ref, g2_ref, b2_ref,
                 w2i_ref, w2o_ref, g3_ref, b3_ref, wg_ref, wp_ref, g4_ref, b4_ref, out_ref):
    x2 = []
    for rows in _SUBTILES:
        mix = (_dot(oh_ref[rows, :], wo_ref[0:HG_WIDTH, :])
               + _dot(os_ref[rows, :], wo_ref[HG_WIDTH:, :]))
        x2.append(_layer_norm(DEEPNORM_ALPHA * x1_ref[rows, :] + mix, g2_ref[...], b2_ref[...]))
    x3 = [_ffn_ln(x, w2i_ref, w2o_ref, g3_ref[...], b3_ref[...]) for x in x2]
    for rows, x in zip(_SUBTILES, x3):
        gate = _sigmoid(_dot(x.astype(BF16), wg_ref[...]))
        ple = gate * _dot(p_ref[rows, :].astype(BF16), wp_ref[...])
        out_ref[rows, :] = _layer_norm(DEEPNORM_ALPHA * x + ple, g4_ref[...], b4_ref[...])


def _post_call(oh, osd, x1, p2d, wo, g2, b2, w2i, w2o, g3, b3, wg, wp, g4, b4):
    n = x1.shape[0]
    weights = (wo, g2, b2, w2i, w2o, g3, b3, wg, wp, g4, b4)
    return pl.pallas_call(
        _post_kernel,
        grid=(n // TOKEN_TILE,),
        in_specs=[_rows(HG_WIDTH), _rows(SSD_WIDTH), _rows(D_MODEL), _rows(PLE_DIM)]
        + [_resident(w.shape) for w in weights],
        out_specs=_rows(D_MODEL),
        out_shape=jax.ShapeDtypeStruct((n, D_MODEL), F32),
        compiler_params=pltpu.CompilerParams(
            dimension_semantics=("arbitrary",), vmem_limit_bytes=V7X_VMEM_LIMIT_BYTES),
        name="post_outproj_ffn_ple",
    )(oh, osd, x1, p2d, *weights)


def _row(v):
    return v.reshape(1, -1).astype(F32)


def kernel(x, p, ffn1_w_in, ffn1_w_out, ln1_g, ln1_b, w_in_mix, hgrn_lb_logits, hgrn_norm_g,
           ssd_conv_w, ssd_conv_b, ssd_dt_bias, ssd_a_log, ssd_d, ssd_norm_g, w_out_mix,
           ln2_g, ln2_b, ffn2_w_in, ffn2_w_out, ln3_g, ln3_b, ple_w_proj, ple_w_gate,
           ln4_g, ln4_b):
    bsz, seqlen, _ = x.shape
    n = bsz * seqlen
    assert DEPTH == 1 and n % TOKEN_TILE == 0
    assert seqlen % SEQ_TILE == 0 and seqlen % HGRN_TILE == 0
    h = x.reshape(n, D_MODEL)
    for i in range(DEPTH):
        pad = MIX_COLS_PADDED - w_in_mix.shape[-1]
        wmix = jnp.pad(w_in_mix[i], ((0, 0), (0, pad))).astype(BF16)
        x1, x1b = _pre_call(h, ffn1_w_in[i].astype(BF16), ffn1_w_out[i].astype(BF16),
                            _row(ln1_g[i]), _row(ln1_b[i]))
        hq, hf, hi, hg, sz, sxbc, sdt = _proj_call(x1b, wmix)

        def seq(t):
            return t.reshape(bsz, seqlen, t.shape[-1])

        o_h = _hgrn_call(hgrn_lb_logits.astype(F32), _row(hgrn_norm_g[i]),
                         seq(hq), seq(hf), seq(hi), seq(hg))
        head_pad = DT_PAD - SSD_HEADS
        o_s = _ssd_call(
            ssd_conv_w[i].astype(F32), _row(ssd_conv_b[i]),
            jnp.pad(_row(ssd_dt_bias[i]), ((0, 0), (0, head_pad))),
            jnp.pad(_row(ssd_a_log[i]), ((0, 0), (0, head_pad))),
            _row(jnp.repeat(ssd_d[i], SSD_HEADDIM)), _row(ssd_norm_g[i]),
            seq(sz), seq(sxbc), seq(sdt))
        h = _post_call(
            o_h.reshape(n, HG_WIDTH), o_s.reshape(n, SSD_WIDTH), x1, p[i].reshape(n, PLE_DIM),
            w_out_mix[i].astype(BF16), _row(ln2_g[i]), _row(ln2_b[i]),
            ffn2_w_in[i].astype(BF16), ffn2_w_out[i].astype(BF16), _row(ln3_g[i]), _row(ln3_b[i]),
            ple_w_gate[i].astype(BF16), ple_w_proj[i].astype(BF16), _row(ln4_g[i]), _row(ln4_b[i]))
    return h.reshape(bsz, seqlen, D_MODEL)
```
